```python
import math
import jax, jax.numpy as jnp
from jax import lax
import numpy as np

D_MODEL = 1024
BATCH = 16
SEQ = 4096
DEPTH = 1

D_CONV = D_MODEL
CONV_WIDTH = 31
N_HEADS = 8
HEAD_DIM = 64
V_DIM = 2 * HEAD_DIM
ROT_DIM = HEAD_DIM // 4
ROPE_THETA = 500000.0
Q_BLOCK = 128
QK_W = N_HEADS * 2 * HEAD_DIM
V_W = N_HEADS * V_DIM
N_EXPERTS = 32
TOP_K = 4
D_FF = D_MODEL
SWIGLU_ALPHA = 1.702
SWIGLU_LIMIT = 7.0
EPS = 1e-6
IN_COLS = 2 * D_CONV + 2 * QK_W + V_W + 2 * D_MODEL

kernel_name = 'hybrid_conformer_diffattn_moe_block'


def rms_norm(x, g):
    xf = x.astype(jnp.float32)
    y = xf * lax.rsqrt(jnp.mean(xf * xf, axis=-1, keepdims=True) + EPS)
    return (y * g.astype(jnp.float32)).astype(x.dtype)


def layer_norm(x, g, b):
    xf = x.astype(jnp.float32)
    mu = jnp.mean(xf, axis=-1, keepdims=True)
    var = jnp.mean(jnp.square(xf - mu), axis=-1, keepdims=True)
    y = (xf - mu) * lax.rsqrt(var + EPS)
    return (y * g.astype(jnp.float32) + b.astype(jnp.float32)).astype(x.dtype)


def rope_tables(positions):
    inv = ROPE_THETA ** (-jnp.arange(0, ROT_DIM, 2, dtype=jnp.float32) / ROT_DIM)
    ang = positions.astype(jnp.float32)[..., None] * inv
    return jnp.cos(ang)[:, :, None, None, :], jnp.sin(ang)[:, :, None, None, :]


def partial_rope(t, cos, sin):
    rot, rest = t[..., :ROT_DIM], t[..., ROT_DIM:]
    x1, x2 = rot[..., :ROT_DIM // 2], rot[..., ROT_DIM // 2:]
    rot = jnp.concatenate([x1 * cos - x2 * sin, x2 * cos + x1 * sin], axis=-1)
    return jnp.concatenate([rot.astype(t.dtype), rest], axis=-1)


def conformer_conv(u_glu, conv_w, conv_b, ln_g, ln_b, w_out, b_out):
    a, gt = jnp.split(u_glu, 2, axis=-1)
    u = a * jax.nn.sigmoid(gt)
    u = lax.conv_general_dilated(
        u, conv_w[:, None, :], window_strides=(1,),
        padding=[(CONV_WIDTH - 1, 0)],
        dimension_numbers=('NWC', 'WIO', 'NWC'),
        feature_group_count=D_CONV) + conv_b
    u = jax.nn.silu(layer_norm(u, ln_g, ln_b))
    return u @ w_out + b_out


def diff_attention(q, k, v, lam, lam_init, subln_g):
    B, S = q.shape[0], q.shape[1]
    nb = S // Q_BLOCK
    scale = HEAD_DIM ** -0.5
    qb = q.reshape(B, nb, Q_BLOCK, N_HEADS, 2, HEAD_DIM).transpose(1, 0, 2, 3, 4, 5)
    k_pos = jnp.arange(S)

    def block(args):
        qi, i = args
        s = jnp.einsum('bqhcd,bkhcd->bhcqk', qi, k,
                       preferred_element_type=jnp.float32) * scale
        q_pos = i * Q_BLOCK + jnp.arange(Q_BLOCK)
        causal = k_pos[None, :] <= q_pos[:, None]
        p = jax.nn.softmax(jnp.where(causal, s, -jnp.inf), axis=-1)
        a = p[:, :, 0] - lam * p[:, :, 1]
        return jnp.einsum('bhqk,bkhe->bqhe', a.astype(v.dtype), v)

    o = lax.map(block, (qb, jnp.arange(nb)))
    o = o.transpose(1, 0, 2, 3, 4).reshape(B, S, N_HEADS, V_DIM)
    o = rms_norm(o, subln_g) * (1.0 - lam_init)
    return o.reshape(B, S, V_W)


def mixer(h, cos, sin, w_in, conv_w, conv_b, conv_ln_g, conv_ln_b, w_conv_out, b_conv_out,
          q_norm_g, k_norm_g, lambda_q1, lambda_k1, lambda_q2, lambda_k2, lam_init,
          subln_g, w_attn_out, b_gate, w_o):
    B, S = h.shape[0], h.shape[1]
    u = h @ w_in
    c0 = 2 * D_CONV
    u_conv, q, k, v, g = jnp.split(
        u, [c0, c0 + QK_W, c0 + 2 * QK_W, c0 + 2 * QK_W + V_W], axis=-1)
    y_conv = conformer_conv(u_conv, conv_w, conv_b, conv_ln_g, conv_ln_b, w_conv_out, b_conv_out)
    q = partial_rope(rms_norm(q.reshape(B, S, N_HEADS, 2, HEAD_DIM), q_norm_g), cos, sin)
    k = partial_rope(rms_norm(k.reshape(B, S, N_HEADS, 2, HEAD_DIM), k_norm_g), cos, sin)
    v = v.reshape(B, S, N_HEADS, V_DIM)
    lam = (jnp.exp(jnp.sum(lambda_q1 * lambda_k1).astype(jnp.float32))
           - jnp.exp(jnp.sum(lambda_q2 * lambda_k2).astype(jnp.float32)) + lam_init)
    y_attn = diff_attention(q, k, v, lam, lam_init, subln_g) @ w_attn_out
    g_conv, g_attn = jnp.split(jax.nn.sigmoid(g + b_gate), 2, axis=-1)
    return (g_conv * y_conv + g_attn * y_attn) @ w_o


def moe(h, w_router, b_router, w_up, b_up, w_down, b_down):
    B, S, D = h.shape
    t = h.reshape(B * S, D)
    logits = (t @ w_router + b_router).astype(jnp.float32)
    top_vals, top_idx = lax.top_k(logits, TOP_K)
    wts = jax.nn.softmax(top_vals, axis=-1)
    combine = jnp.sum(jax.nn.one_hot(top_idx, N_EXPERTS, dtype=jnp.float32) * wts[..., None],
                      axis=1).astype(t.dtype)
    y = jnp.zeros_like(t)
    for e in range(N_EXPERTS):
        gu = t @ w_up[e] + b_up[e]
        glu = jnp.minimum(gu[:, 0::2], SWIGLU_LIMIT)
        lin = jnp.clip(gu[:, 1::2], -SWIGLU_LIMIT, SWIGLU_LIMIT)
        act = glu * jax.nn.sigmoid(SWIGLU_ALPHA * glu) * (lin + 1.0)
        y = y + combine[:, e:e + 1] * (act @ w_down[e] + b_down[e])
    return y.reshape(B, S, D)


def setup_inputs(seed: int = 0) -> dict:
    key = jax.random.key(seed)
    ks = jax.random.split(key, 32)
    f32 = jnp.float32
    L = DEPTH

    def nrm(k, shape, scale):
        return jax.random.normal(k, shape, f32) * scale

    return {
        'x': nrm(ks[0], (BATCH, SEQ, D_MODEL), 1.0),
        'c': nrm(ks[1], (BATCH, D_MODEL), 1.0),
        'positions': (jnp.arange(SEQ, dtype=jnp.int32)[None, :]
                      + jax.random.randint(ks[2], (BATCH, 1), 0, 1024, dtype=jnp.int32)),
        'w_ada': nrm(ks[3], (L, D_MODEL, 6 * D_MODEL), 0.5 * D_MODEL ** -0.5),
        'b_ada': nrm(ks[4], (L, 6 * D_MODEL), 0.02),
        'norm1_g': 1.0 + nrm(ks[5], (L, D_MODEL), 0.1),
        'w_in': nrm(ks[6], (L, D_MODEL, IN_COLS), D_MODEL ** -0.5),
        'conv_w': nrm(ks[7], (L, CONV_WIDTH, D_CONV), CONV_WIDTH ** -0.5),
        'conv_b': nrm(ks[8], (L, D_CONV), 0.02),
        'conv_ln_g': 1.0 + nrm(ks[9], (L, D_CONV), 0.1),
        'conv_ln_b': nrm(ks[10], (L, D_CONV), 0.02),
        'w_conv_out': nrm(ks[11], (L, D_CONV, D_MODEL), D_CONV ** -0.5),
        'b_conv_out': nrm(ks[12], (L, D_MODEL), 0.02),
        'q_norm_g': 1.0 + nrm(ks[13], (L, HEAD_DIM), 0.1),
        'k_norm_g': 1.0 + nrm(ks[14], (L, HEAD_DIM), 0.1),
        'lambda_q1': nrm(ks[15], (L, HEAD_DIM), 0.1),
        'lambda_k1': nrm(ks[16], (L, HEAD_DIM), 0.1),
        'lambda_q2': nrm(ks[17], (L, HEAD_DIM), 0.1),
        'lambda_k2': nrm(ks[18], (L, HEAD_DIM), 0.1),
        'subln_g': 1.0 + nrm(ks[19], (L, V_DIM), 0.1),
        'w_attn_out': nrm(ks[20], (L, V_W, D_MODEL), V_W ** -0.5),
        'b_gate': nrm(ks[21], (L, 2 * D_MODEL), 0.02),
        'w_o': nrm(ks[22], (L, D_MODEL, D_MODEL), D_MODEL ** -0.5),
        'norm2_g': 1.0 + nrm(ks[23], (L, D_MODEL), 0.1),
        'w_router': nrm(ks[24], (L, D_MODEL, N_EXPERTS), D_MODEL ** -0.5),
        'b_router': nrm(ks[25], (L, N_EXPERTS), 0.01),
        'w_up': nrm(ks[26], (L, N_EXPERTS, D_MODEL, 2 * D_FF), D_MODEL ** -0.5),
        'b_up': nrm(ks[27], (L, N_EXPERTS, 2 * D_FF), 0.02),
        'w_down': nrm(ks[28], (L, N_EXPERTS, D_FF, D_MODEL), D_FF ** -0.5),
        'b_down': nrm(ks[29], (L, N_EXPERTS, D_MODEL), 0.02),
    }


def reference(x, c, positions, w_ada, b_ada, norm1_g, w_in, conv_w, conv_b, conv_ln_g, conv_ln_b,
              w_conv_out, b_conv_out, q_norm_g, k_norm_g, lambda_q1, lambda_k1, lambda_q2, lambda_k2,
              subln_g, w_attn_out, b_gate, w_o, norm2_g, w_router, b_router, w_up, b_up, w_down, b_down):
    cos, sin = rope_tables(positions)
    c_act = jax.nn.silu(c)
    for l in range(DEPTH):
        lam_init = 0.8 - 0.6 * math.exp(-0.3 * l)
        mod = (c_act @ w_ada[l] + b_ada[l])[:, None, :]
        shift1, scale1, gate1, shift2, scale2, gate2 = jnp.split(mod, 6, axis=-1)
        h = rms_norm(x, norm1_g[l]) * (1.0 + scale1) + shift1
        x = x + gate1 * mixer(h, cos, sin, w_in[l], conv_w[l], conv_b[l], conv_ln_g[l], conv_ln_b[l],
                              w_conv_out[l], b_conv_out[l], q_norm_g[l], k_norm_g[l],
                              lambda_q1[l], lambda_k1[l], lambda_q2[l], lambda_k2[l], lam_init,
                              subln_g[l], w_attn_out[l], b_gate[l], w_o[l])
        h = rms_norm(x, norm2_g[l]) * (1.0 + scale2) + shift2
        x = x + gate2 * moe(h, w_router[l], b_router[l], w_up[l], b_up[l], w_down[l], b_down[l])
    return x
```

```python
import functools
import math

import numpy as np
import jax
import jax.numpy as jnp
from jax import lax
from jax.experimental import pallas as pl
from jax.experimental.pallas import tpu as pltpu

EPS = 1e-6
ROPE_THETA = 500000.0
TOP_K = 4
SWIGLU_ALPHA = 1.702
SWIGLU_LIMIT = 7.0
LANES = 128
CONV_HALO = 32
NEG_BIG = -1e30
VMEM_LIMIT = 56 * 1024 * 1024

f32 = jnp.float32
bf16 = jnp.bfloat16


def _const_spec(shape):
    zeros = (0,) * len(shape)
    return pl.BlockSpec(shape, lambda *_: zeros, pipeline_mode=pl.Buffered(1))


def _params(sem):
    return pltpu.CompilerParams(dimension_semantics=sem, vmem_limit_bytes=VMEM_LIMIT)


def _rms_rows(x):
    return x * lax.rsqrt(jnp.mean(x * x, axis=-1, keepdims=True) + EPS)


def _mod_h(x, mod_ref, g_ref, shift_i, scale_i):
    shift = mod_ref[0, shift_i:shift_i + 1, :]
    scale = mod_ref[0, scale_i:scale_i + 1, :]
    return _rms_rows(x) * g_ref[...] * (1.0 + scale) + shift


def _ada_kernel(c_ref, w_ref, b_ref, o_ref):
    c = c_ref[...]
    ca = c * jax.nn.sigmoid(c)
    o_ref[...] = jnp.dot(ca.astype(bf16), w_ref[...].astype(bf16),
                         preferred_element_type=f32) + b_ref[...]


def _ada(c, w_ada, b_ada):
    B, D = c.shape
    N = w_ada.shape[1]
    bn = D
    assert N % bn == 0
    return pl.pallas_call(
        _ada_kernel,
        grid=(N // bn,),
        in_specs=[pl.BlockSpec((B, D), lambda j: (0, 0)),
                  pl.BlockSpec((D, bn), lambda j: (0, j)),
                  pl.BlockSpec((1, bn), lambda j: (0, j))],
        out_specs=pl.BlockSpec((B, bn), lambda j: (0, j)),
        out_shape=jax.ShapeDtypeStruct((B, N), f32),
        compiler_params=_params(("arbitrary",)),
        name="ada",
    )(c, w_ada, b_ada.reshape(1, N))


def _conv_kernel(x_ref, mod_ref, g1_ref, wa_ref, wgt_ref, wg_ref, bg_ref, cw_ref, cb_ref,
                 lng_ref, lnb_ref, wco_ref, bco_ref, gy_ref, ga_ref, ebuf, cbuf, *, width):
    ts, d = x_ref.shape
    i = pl.program_id(1)

    @pl.when(i == 0)
    def _():
        ebuf[0:CONV_HALO, :] = jnp.zeros((CONV_HALO, d), f32)

    hb = _mod_h(x_ref[...], mod_ref, g1_ref, 0, 1).astype(bf16)
    a = jnp.dot(hb, wa_ref[...], preferred_element_type=f32)
    gt = jnp.dot(hb, wgt_ref[...], preferred_element_type=f32)
    ebuf[CONV_HALO:CONV_HALO + ts, :] = a * jax.nn.sigmoid(gt)

    rc = min(32, ts)
    cc = min(512, d)
    base = CONV_HALO - (width - 1)
    for r0 in range(0, ts, rc):
        for c0 in range(0, d, cc):
            acc = jnp.broadcast_to(cb_ref[:, c0:c0 + cc], (rc, cc))
            for j in range(width):
                acc = acc + cw_ref[j:j + 1, c0:c0 + cc] * ebuf[base + j + r0:base + j + r0 + rc, c0:c0 + cc]
            cbuf[r0:r0 + rc, c0:c0 + cc] = acc
    ebuf[0:CONV_HALO, :] = ebuf[ts:ts + CONV_HALO, :]

    cv = cbuf[...]
    mu = jnp.mean(cv, axis=-1, keepdims=True)
    xc = cv - mu
    var = jnp.mean(xc * xc, axis=-1, keepdims=True)
    ln = xc * lax.rsqrt(var + EPS) * lng_ref[...] + lnb_ref[...]
    sw = ln * jax.nn.sigmoid(ln)
    y_conv = jnp.dot(sw.astype(bf16), wco_ref[...], preferred_element_type=f32) + bco_ref[...]

    g = jax.nn.sigmoid(jnp.dot(hb, wg_ref[...], preferred_element_type=f32) + bg_ref[...])
    gy_ref[...] = (g[:, :d] * y_conv).astype(bf16)
    ga_ref[...] = g[:, d:].astype(bf16)


def _conv_branch(x2, mod3, g1, wa, wgt, wg, bg, cw, cb, lng, lnb, wco, bco, B, S):
    T, D = x2.shape
    width = cw.shape[0]
    ts = min(256, S)
    ns = S // ts
    row = lambda b, i: (b * ns + i, 0)
    return pl.pallas_call(
        functools.partial(_conv_kernel, width=width),
        grid=(B, ns),
        in_specs=[pl.BlockSpec((ts, D), row),
                  pl.BlockSpec((1, 6, D), lambda b, i: (b, 0, 0)),
                  _const_spec((1, D)), _const_spec((D, D)), _const_spec((D, D)),
                  _const_spec((D, 2 * D)), _const_spec((1, 2 * D)),
                  _const_spec((width, D)), _const_spec((1, D)), _const_spec((1, D)),
                  _const_spec((1, D)), _const_spec((D, D)), _const_spec((1, D))],
        out_specs=[pl.BlockSpec((ts, D), row), pl.BlockSpec((ts, D), row)],
        out_shape=[jax.ShapeDtypeStruct((T, D), bf16), jax.ShapeDtypeStruct((T, D), bf16)],
        scratch_shapes=[pltpu.VMEM((CONV_HALO + ts, D), f32), pltpu.VMEM((ts, D), f32)],
        compiler_params=_params(("arbitrary", "arbitrary")),
        name="conv_branch",
    )(x2, mod3, g1, wa, wgt, wg, bg, cw, cb, lng, lnb, wco, bco)


def _qkv_kernel(x_ref, mod_ref, g1_ref, pos_ref, wq_ref, wk_ref, wv_ref, qg_ref, kg_ref,
                inv_ref, m1_ref, m2_ref, bd_ref, q_ref, k_ref, v_ref, *, head_dim):
    hb = _mod_h(x_ref[...], mod_ref, g1_ref, 0, 1).astype(bf16)
    v_ref[...] = jnp.dot(hb, wv_ref[...], preferred_element_type=f32).astype(bf16)

    ang = pos_ref[...] * inv_ref[...]
    cosv = jnp.cos(ang)
    sinv = jnp.sin(ang)
    s_lo = sinv * m1_ref[...]
    s_hi = sinv * m2_ref[...]
    half = head_dim // 8
    bd = bd_ref[...]
    inv_hd = 1.0 / head_dim

    def norm_rope(w_ref, g_ref, o_ref):
        t = jnp.dot(hb, w_ref[...], preferred_element_type=f32)
        for c0 in range(0, t.shape[1], LANES):
            tb = t[:, c0:c0 + LANES]
            ss = jnp.dot((tb * tb).astype(bf16), bd, preferred_element_type=f32)
            tn = tb * lax.rsqrt(ss * inv_hd + EPS) * g_ref[...]
            rot = (tn * cosv + pltpu.roll(tn, LANES - half, 1) * s_lo
                   + pltpu.roll(tn, half, 1) * s_hi)
            o_ref[:, c0:c0 + LANES] = rot.astype(bf16)

    norm_rope(wq_ref, qg_ref, q_ref)
    norm_rope(wk_ref, kg_ref, k_ref)


def _qkv(x2, mod3, g1, pos, wq, wk, wv, qg, kg, inv, m1, m2, bd, B, S, head_dim):
    T, D = x2.shape
    qkw, vw = wq.shape[1], wv.shape[1]
    ts = min(512, S)
    ns = S // ts
    row = lambda b, i: (b * ns + i, 0)
    return pl.pallas_call(
        functools.partial(_qkv_kernel, head_dim=head_dim),
        grid=(B, ns),
        in_specs=[pl.BlockSpec((ts, D), row),
                  pl.BlockSpec((1, 6, D), lambda b, i: (b, 0, 0)),
                  _const_spec((1, D)),
                  pl.BlockSpec((ts, 1), row),
                  _const_spec((D, qkw)), _const_spec((D, qkw)), _const_spec((D, vw)),
                  _const_spec((1, LANES)), _const_spec((1, LANES)), _const_spec((1, LANES)),
                  _const_spec((1, LANES)), _const_spec((1, LANES)), _const_spec((LANES, LANES))],
        out_specs=[pl.BlockSpec((ts, qkw), row), pl.BlockSpec((ts, qkw), row),
                   pl.BlockSpec((ts, vw), row)],
        out_shape=[jax.ShapeDtypeStruct((T, qkw), bf16), jax.ShapeDtypeStruct((T, qkw), bf16),
                   jax.ShapeDtypeStruct((T, vw), bf16)],
        compiler_params=_params(("arbitrary", "arbitrary")),
        name="qkv",
    )(x2, mod3, g1, pos, wq, wk, wv, qg, kg, inv, m1, m2, bd)


def _attn_kernel(q_ref, k_ref, v_ref, lq1_ref, lk1_ref, lq2_ref, lk2_ref, sg_ref, o_ref,
                 m_ref, l_ref, acc_ref, *, lam_init, head_dim):
    tq = q_ref.shape[0]
    tk = tq
    i = pl.program_id(2)

    q = q_ref[...]
    lane = lax.broadcasted_iota(jnp.int32, q.shape, 1)
    zero = jnp.zeros_like(q)
    qs = jnp.concatenate([jnp.where(lane < head_dim, q, zero),
                          jnp.where(lane >= head_dim, q, zero)], axis=0)

    m_ref[...] = jnp.full(m_ref.shape, -jnp.inf, f32)
    l_ref[...] = jnp.zeros(l_ref.shape, f32)
    acc_ref[...] = jnp.zeros(acc_ref.shape, f32)

    def step(j, masked):
        start = pl.multiple_of(j * tk, tk)
        kj = k_ref[pl.ds(start, tk), :]
        vj = v_ref[pl.ds(start, tk), :]
        s = lax.dot_general(qs, kj, (((1,), (1,)), ((), ())), preferred_element_type=f32)
        if masked:
            r = lax.broadcasted_iota(jnp.int32, (tq, tk), 0)
            c = lax.broadcasted_iota(jnp.int32, (tq, tk), 1)
            keep = c <= r
            keep = jnp.concatenate([keep, keep], axis=0)
            s = jnp.where(keep, s, -jnp.inf)
        m_prev = m_ref[...]
        m_new = jnp.maximum(m_prev, jnp.max(s, axis=-1, keepdims=True))
        alpha = jnp.exp(m_prev - m_new)
        p = jnp.exp(s - m_new)
        l_ref[...] = alpha * l_ref[...] + jnp.sum(p, axis=-1, keepdims=True)
        acc_ref[...] = alpha * acc_ref[...] + jnp.dot(p.astype(bf16), vj,
                                                      preferred_element_type=f32)
        m_ref[...] = m_new

    def body(j, carry):
        step(j, False)
        return carry

    lax.fori_loop(0, i, body, 0)
    step(i, True)

    lam = (jnp.exp(jnp.sum(lq1_ref[...] * lk1_ref[...], axis=-1, keepdims=True))
           - jnp.exp(jnp.sum(lq2_ref[...] * lk2_ref[...], axis=-1, keepdims=True)) + lam_init)
    o_all = acc_ref[...] / l_ref[...]
    o = o_all[:tq] - lam * o_all[tq:]
    o = _rms_rows(o) * sg_ref[...] * (1.0 - lam_init)
    o_ref[...] = o.astype(bf16)


def _attention(q, k, v, lq1, lk1, lq2, lk2, sg, B, S, H, head_dim, lam_init):
    T = q.shape[0]
    tq = min(512, S)
    nq = S // tq
    vd = sg.shape[1]
    return pl.pallas_call(
        functools.partial(_attn_kernel, lam_init=lam_init, head_dim=head_dim),
        grid=(B, H, nq),
        in_specs=[pl.BlockSpec((tq, 2 * head_dim), lambda b, h, i: (b * nq + i, h)),
                  pl.BlockSpec((S, 2 * head_dim), lambda b, h, i: (b, h)),
                  pl.BlockSpec((S, vd), lambda b, h, i: (b, h)),
                  _const_spec((1, head_dim)), _const_spec((1, head_dim)),
                  _const_spec((1, head_dim)), _const_spec((1, head_dim)),
                  _const_spec((1, vd))],
        out_specs=pl.BlockSpec((tq, vd), lambda b, h, i: (b * nq + i, h)),
        out_shape=jax.ShapeDtypeStruct((T, H * vd), bf16),
        scratch_shapes=[pltpu.VMEM((2 * tq, 1), f32), pltpu.VMEM((2 * tq, 1), f32),
                        pltpu.VMEM((2 * tq, vd), f32)],
        compiler_params=_params(("arbitrary", "arbitrary", "arbitrary")),
        name="diff_attn",
    )(q, k, v, lq1, lk1, lq2, lk2, sg)


def _post_kernel(ao_ref, gy_ref, ga_ref, x_ref, mod_ref, wao_ref, wo_ref, g2_ref, wr_ref, br_ref,
                 x1_ref, h2_ref, cw_ref):
    ya = jnp.dot(ao_ref[...], wao_ref[...], preferred_element_type=f32)
    mix_in = gy_ref[...].astype(f32) + ga_ref[...].astype(f32) * ya
    mix = jnp.dot(mix_in.astype(bf16), wo_ref[...], preferred_element_type=f32)
    gate1 = mod_ref[0, 2:3, :]
    x1 = x_ref[...] + gate1 * mix
    x1_ref[...] = x1
    h2 = _mod_h(x1, mod_ref, g2_ref, 3, 4).astype(bf16)
    h2_ref[...] = h2

    logits = jnp.dot(h2, wr_ref[...], preferred_element_type=f32) + br_ref[...]
    lane = lax.broadcasted_iota(jnp.int32, logits.shape, 1)
    work = logits
    sel, vals = [], []
    for _ in range(TOP_K):
        mx = jnp.max(work, axis=-1, keepdims=True)
        ek = jnp.min(jnp.where(work == mx, lane, LANES), axis=-1, keepdims=True)
        sel.append(ek)
        vals.append(mx)
        work = jnp.where(lane == ek, NEG_BIG, work)
    ex = [jnp.exp(vk - vals[0]) for vk in vals]
    inv_den = 1.0 / functools.reduce(lambda a, b: a + b, ex)
    cw = jnp.zeros(logits.shape, f32)
    for ek, xk in zip(sel, ex):
        cw = jnp.where(lane == ek, xk * inv_den, cw)
    cw_ref[...] = cw


def _post(ao, gy, ga, x2, mod3, wao, wo, g2, wr, br, S):
    T, D = x2.shape
    vw = ao.shape[1]
    tm = min(512, S)
    per_b = S // tm
    row = lambda t: (t, 0)
    return pl.pallas_call(
        _post_kernel,
        grid=(T // tm,),
        in_specs=[pl.BlockSpec((tm, vw), row), pl.BlockSpec((tm, D), row),
                  pl.BlockSpec((tm, D), row), pl.BlockSpec((tm, D), row),
                  pl.BlockSpec((1, 6, D), lambda t: (t // per_b, 0, 0)),
                  _const_spec((vw, D)), _const_spec((D, D)), _const_spec((1, D)),
                  _const_spec((D, LANES)), _const_spec((1, LANES))],
        out_specs=[pl.BlockSpec((tm, D), row), pl.BlockSpec((tm, D), row),
                   pl.BlockSpec((tm, LANES), row)],
        out_shape=[jax.ShapeDtypeStruct((T, D), f32), jax.ShapeDtypeStruct((T, D), bf16),
                   jax.ShapeDtypeStruct((T, LANES), f32)],
        compiler_params=_params(("arbitrary",)),
        name="post_router",
    )(ao, gy, ga, x2, mod3, wao, wo, g2, wr, br)


def _moe_dense_kernel(h2_ref, cw_ref, x1_ref, mod_ref, wg_ref, wl_ref, bgl_ref, bll_ref,
                      wd_ref, bd_ref, o_ref, acc_ref):
    e = pl.program_id(1)

    @pl.when(e == 0)
    def _():
        acc_ref[...] = jnp.zeros(acc_ref.shape, f32)

    h2 = h2_ref[...]
    glu = jnp.minimum(jnp.dot(h2, wg_ref[0], preferred_element_type=f32) + bgl_ref[0], SWIGLU_LIMIT)
    lin = jnp.clip(jnp.dot(h2, wl_ref[0], preferred_element_type=f32) + bll_ref[0],
                   -SWIGLU_LIMIT, SWIGLU_LIMIT)
    act = glu * jax.nn.sigmoid(SWIGLU_ALPHA * glu) * (lin + 1.0)
    y = jnp.dot(act.astype(bf16), wd_ref[0], preferred_element_type=f32) + bd_ref[0]
    cw = cw_ref[...]
    lane = lax.broadcasted_iota(jnp.int32, cw.shape, 1)
    ce = jnp.sum(jnp.where(lane == e, cw, 0.0), axis=-1, keepdims=True)
    acc_ref[...] += ce * y

    @pl.when(e == pl.num_programs(1) - 1)
    def _():
        o_ref[...] = x1_ref[...] + mod_ref[0, 5:6, :] * acc_ref[...]


def _moe_dense(h2, cw, x1, mod3, wg, wl, bgl, bll, wd, bdn, S):
    T, D = x1.shape
    E, _, F = wg.shape
    tm = min(512, S)
    per_b = S // tm
    row = lambda t, e: (t, 0)
    ex = lambda t, e: (e, 0, 0)
    return pl.pallas_call(
        _moe_dense_kernel,
        grid=(T // tm, E),
        in_specs=[pl.BlockSpec((tm, D), row), pl.BlockSpec((tm, LANES), row),
                  pl.BlockSpec((tm, D), row),
                  pl.BlockSpec((1, 6, D), lambda t, e: (t // per_b, 0, 0)),
                  pl.BlockSpec((1, D, F), ex), pl.BlockSpec((1, D, F), ex),
                  pl.BlockSpec((1, 1, F), ex), pl.BlockSpec((1, 1, F), ex),
                  pl.BlockSpec((1, F, D), ex), pl.BlockSpec((1, 1, D), ex)],
        out_specs=pl.BlockSpec((tm, D), row),
        out_shape=jax.ShapeDtypeStruct((T, D), f32),
        scratch_shapes=[pltpu.VMEM((tm, D), f32)],
        compiler_params=_params(("arbitrary", "arbitrary")),
        name="moe_dense",
    )(h2, cw, x1, mod3, wg, wl, bgl, bll, wd, bdn)


def _rope_lane_tables(head_dim):
    rot = head_dim // 4
    half = rot // 2
    lane = np.arange(LANES) % head_dim
    inv = np.where(lane < rot, ROPE_THETA ** (-(2.0 * (lane % half)) / rot), 0.0)
    m1 = np.where(lane < half, -1.0, 0.0)
    m2 = np.where((lane >= half) & (lane < rot), 1.0, 0.0)
    blk = np.arange(LANES) // head_dim
    bd = (blk[:, None] == blk[None, :]).astype(np.float32)
    as_row = lambda a: jnp.asarray(a, f32).reshape(1, LANES)
    return as_row(inv), as_row(m1), as_row(m2), jnp.asarray(bd, bf16)


def kernel(x, c, positions, w_ada, b_ada, norm1_g, w_in, conv_w, conv_b, conv_ln_g, conv_ln_b,
           w_conv_out, b_conv_out, q_norm_g, k_norm_g, lambda_q1, lambda_k1, lambda_q2, lambda_k2,
           subln_g, w_attn_out, b_gate, w_o, norm2_g, w_router, b_router, w_up, b_up, w_down, b_down):
    B, S, D = x.shape
    depth = w_ada.shape[0]
    hd = q_norm_g.shape[-1]
    vd = subln_g.shape[-1]
    vw = w_attn_out.shape[1]
    H = vw // vd
    qkw = H * 2 * hd
    E = w_router.shape[-1]
    F = w_down.shape[2]
    dc = conv_w.shape[-1]
    assert 2 * hd == LANES and vd == LANES and dc == D and E <= LANES
    T = B * S

    inv, m1, m2, bd = _rope_lane_tables(hd)
    pos = positions.astype(f32).reshape(T, 1)
    c_in = c
    x2 = x.reshape(T, D)
    for l in range(depth):
        lam_init = 0.8 - 0.6 * math.exp(-0.3 * l)
        mod3 = _ada(c_in, w_ada[l], b_ada[l]).reshape(B, 6, D)
        row = lambda a: a.reshape(1, -1)
        wi = w_in[l].astype(bf16)
        c0 = 2 * dc
        wa, wgt = wi[:, :dc], wi[:, dc:c0]
        wq, wk = wi[:, c0:c0 + qkw], wi[:, c0 + qkw:c0 + 2 * qkw]
        wv = wi[:, c0 + 2 * qkw:c0 + 2 * qkw + vw]
        wg = wi[:, c0 + 2 * qkw + vw:]
        g1 = row(norm1_g[l])

        gy, ga = _conv_branch(x2, mod3, g1, wa, wgt, wg, row(b_gate[l]), conv_w[l], row(conv_b[l]),
                              row(conv_ln_g[l]), row(conv_ln_b[l]), w_conv_out[l].astype(bf16),
                              row(b_conv_out[l]), B, S)
        qg = row(jnp.tile(q_norm_g[l], 2) * (hd ** -0.5))
        kg = row(jnp.tile(k_norm_g[l], 2))
        q, k, v = _qkv(x2, mod3, g1, pos, wq, wk, wv, qg, kg, inv, m1, m2, bd, B, S, hd)
        ao = _attention(q, k, v, row(lambda_q1[l]), row(lambda_k1[l]), row(lambda_q2[l]),
                        row(lambda_k2[l]), row(subln_g[l]), B, S, H, hd, lam_init)

        wr = jnp.zeros((D, LANES), f32).at[:, :E].set(w_router[l]).astype(bf16)
        br = jnp.full((1, LANES), NEG_BIG, f32).at[0, :E].set(b_router[l])
        x1, h2, cw = _post(ao, gy, ga, x2, mod3, w_attn_out[l].astype(bf16), w_o[l].astype(bf16),
                           row(norm2_g[l]), wr, br, S)

        wu = w_up[l].astype(bf16)
        x2 = _moe_dense(h2, cw, x1, mod3, wu[:, :, 0::2], wu[:, :, 1::2],
                        b_up[l][:, None, 0::2], b_up[l][:, None, 1::2],
                        w_down[l].astype(bf16), b_down[l][:, None, :], S)
    return x2.reshape(B, S, D)
```

```python
import functools
import math

import numpy as np
import jax
import jax.numpy as jnp
from jax import lax
from jax.experimental import pallas as pl
from jax.experimental.pallas import tpu as pltpu

EPS = 1e-6
ROPE_THETA = 500000.0
TOP_K = 4
SWIGLU_ALPHA = 1.702
SWIGLU_LIMIT = 7.0
LANES = 128
CONV_HALO = 32
NEG_BIG = -1e30
VMEM_LIMIT = 56 * 1024 * 1024

f32 = jnp.float32
bf16 = jnp.bfloat16


def _const_spec(shape):
    zeros = (0,) * len(shape)
    return pl.BlockSpec(shape, lambda *_: zeros, pipeline_mode=pl.Buffered(1))


def _params(sem):
    return pltpu.CompilerParams(dimension_semantics=sem, vmem_limit_bytes=VMEM_LIMIT)


def _rms_rows(x):
    return x * lax.rsqrt(jnp.mean(x * x, axis=-1, keepdims=True) + EPS)


def _mod_h(x, mod_ref, g_ref, shift_i, scale_i):
    shift = mod_ref[0, shift_i:shift_i + 1, :]
    scale = mod_ref[0, scale_i:scale_i + 1, :]
    return _rms_rows(x) * g_ref[...] * (1.0 + scale) + shift


def _ada_kernel(c_ref, w_ref, b_ref, o_ref):
    c = c_ref[...]
    ca = c * jax.nn.sigmoid(c)
    o_ref[...] = jnp.dot(ca.astype(bf16), w_ref[...].astype(bf16),
                         preferred_element_type=f32) + b_ref[...]


def _ada(c, w_ada, b_ada):
    B, D = c.shape
    N = w_ada.shape[1]
    bn = D
    assert N % bn == 0
    return pl.pallas_call(
        _ada_kernel,
        grid=(N // bn,),
        in_specs=[pl.BlockSpec((B, D), lambda j: (0, 0)),
                  pl.BlockSpec((D, bn), lambda j: (0, j)),
                  pl.BlockSpec((1, bn), lambda j: (0, j))],
        out_specs=pl.BlockSpec((B, bn), lambda j: (0, j)),
        out_shape=jax.ShapeDtypeStruct((B, N), f32),
        compiler_params=_params(("arbitrary",)),
        name="ada",
    )(c, w_ada, b_ada.reshape(1, N))


def _conv_kernel(x_ref, mod_ref, g1_ref, wa_ref, wgt_ref, wg_ref, bg_ref, cw_ref, cb_ref,
                 lng_ref, lnb_ref, wco_ref, bco_ref, gy_ref, ga_ref, ebuf, cbuf, *, width):
    ts, d = x_ref.shape
    i = pl.program_id(1)

    @pl.when(i == 0)
    def _():
        ebuf[0:CONV_HALO, :] = jnp.zeros((CONV_HALO, d), f32)

    hb = _mod_h(x_ref[...], mod_ref, g1_ref, 0, 1).astype(bf16)
    a = jnp.dot(hb, wa_ref[...], preferred_element_type=f32)
    gt = jnp.dot(hb, wgt_ref[...], preferred_element_type=f32)
    ebuf[CONV_HALO:CONV_HALO + ts, :] = a * jax.nn.sigmoid(gt)

    rc = min(32, ts)
    cc = min(512, d)
    base = CONV_HALO - (width - 1)
    for r0 in range(0, ts, rc):
        for c0 in range(0, d, cc):
            acc = jnp.broadcast_to(cb_ref[:, c0:c0 + cc], (rc, cc))
            for j in range(width):
                acc = acc + cw_ref[j:j + 1, c0:c0 + cc] * ebuf[base + j + r0:base + j + r0 + rc, c0:c0 + cc]
            cbuf[r0:r0 + rc, c0:c0 + cc] = acc
    ebuf[0:CONV_HALO, :] = ebuf[ts:ts + CONV_HALO, :]

    cv = cbuf[...]
    mu = jnp.mean(cv, axis=-1, keepdims=True)
    xc = cv - mu
    var = jnp.mean(xc * xc, axis=-1, keepdims=True)
    ln = xc * lax.rsqrt(var + EPS) * lng_ref[...] + lnb_ref[...]
    sw = ln * jax.nn.sigmoid(ln)
    y_conv = jnp.dot(sw.astype(bf16), wco_ref[...], preferred_element_type=f32) + bco_ref[...]

    g = jax.nn.sigmoid(jnp.dot(hb, wg_ref[...], preferred_element_type=f32) + bg_ref[...])
    gy_ref[...] = (g[:, :d] * y_conv).astype(bf16)
    ga_ref[...] = g[:, d:].astype(bf16)


def _conv_branch(x2, mod3, g1, wa, wgt, wg, bg, cw, cb, lng, lnb, wco, bco, B, S):
    T, D = x2.shape
    width = cw.shape[0]
    ts = min(256, S)
    ns = S // ts
    row = lambda b, i: (b * ns + i, 0)
    return pl.pallas_call(
        functools.partial(_conv_kernel, width=width),
        grid=(B, ns),
        in_specs=[pl.BlockSpec((ts, D), row),
                  pl.BlockSpec((1, 6, D), lambda b, i: (b, 0, 0)),
                  _const_spec((1, D)), _const_spec((D, D)), _const_spec((D, D)),
                  _const_spec((D, 2 * D)), _const_spec((1, 2 * D)),
                  _const_spec((width, D)), _const_spec((1, D)), _const_spec((1, D)),
                  _const_spec((1, D)), _const_spec((D, D)), _const_spec((1, D))],
        out_specs=[pl.BlockSpec((ts, D), row), pl.BlockSpec((ts, D), row)],
        out_shape=[jax.ShapeDtypeStruct((T, D), bf16), jax.ShapeDtypeStruct((T, D), bf16)],
        scratch_shapes=[pltpu.VMEM((CONV_HALO + ts, D), f32), pltpu.VMEM((ts, D), f32)],
        compiler_params=_params(("arbitrary", "arbitrary")),
        name="conv_branch",
    )(x2, mod3, g1, wa, wgt, wg, bg, cw, cb, lng, lnb, wco, bco)


def _qkv_kernel(x_ref, mod_ref, g1_ref, pos_ref, wq_ref, wk_ref, wv_ref, qg_ref, kg_ref,
                inv_ref, m1_ref, m2_ref, bd_ref, q_ref, k_ref, v_ref, *, head_dim):
    hb = _mod_h(x_ref[...], mod_ref, g1_ref, 0, 1).astype(bf16)
    v_ref[...] = jnp.dot(hb, wv_ref[...], preferred_element_type=f32).astype(bf16)

    ang = pos_ref[...] * inv_ref[...]
    cosv = jnp.cos(ang)
    sinv = jnp.sin(ang)
    s_lo = sinv * m1_ref[...]
    s_hi = sinv * m2_ref[...]
    half = head_dim // 8
    bd = bd_ref[...]
    inv_hd = 1.0 / head_dim

    def norm_rope(w_ref, g_ref, o_ref):
        t = jnp.dot(hb, w_ref[...], preferred_element_type=f32)
        for c0 in range(0, t.shape[1], LANES):
            tb = t[:, c0:c0 + LANES]
            ss = jnp.dot((tb * tb).astype(bf16), bd, preferred_element_type=f32)
            tn = tb * lax.rsqrt(ss * inv_hd + EPS) * g_ref[...]
            rot = (tn * cosv + pltpu.roll(tn, LANES - half, 1) * s_lo
                   + pltpu.roll(tn, half, 1) * s_hi)
            o_ref[:, c0:c0 + LANES] = rot.astype(bf16)

    norm_rope(wq_ref, qg_ref, q_ref)
    norm_rope(wk_ref, kg_ref, k_ref)


def _qkv(x2, mod3, g1, pos, wq, wk, wv, qg, kg, inv, m1, m2, bd, B, S, head_dim):
    T, D = x2.shape
    qkw, vw = wq.shape[1], wv.shape[1]
    ts = min(512, S)
    ns = S // ts
    row = lambda b, i: (b * ns + i, 0)
    return pl.pallas_call(
        functools.partial(_qkv_kernel, head_dim=head_dim),
        grid=(B, ns),
        in_specs=[pl.BlockSpec((ts, D), row),
                  pl.BlockSpec((1, 6, D), lambda b, i: (b, 0, 0)),
                  _const_spec((1, D)),
                  pl.BlockSpec((ts, 1), row),
                  _const_spec((D, qkw)), _const_spec((D, qkw)), _const_spec((D, vw)),
                  _const_spec((1, LANES)), _const_spec((1, LANES)), _const_spec((1, LANES)),
                  _const_spec((1, LANES)), _const_spec((1, LANES)), _const_spec((LANES, LANES))],
        out_specs=[pl.BlockSpec((ts, qkw), row), pl.BlockSpec((ts, qkw), row),
                   pl.BlockSpec((ts, vw), row)],
        out_shape=[jax.ShapeDtypeStruct((T, qkw), bf16), jax.ShapeDtypeStruct((T, qkw), bf16),
                   jax.ShapeDtypeStruct((T, vw), bf16)],
        compiler_params=_params(("arbitrary", "arbitrary")),
        name="qkv",
    )(x2, mod3, g1, pos, wq, wk, wv, qg, kg, inv, m1, m2, bd)


def _attn_kernel(q_ref, k_ref, v_ref, lq1_ref, lk1_ref, lq2_ref, lk2_ref, sg_ref, o_ref,
                 m_ref, l_ref, acc_ref, *, lam_init, head_dim, tk):
    tq = q_ref.shape[0]
    nsub = tq // tk
    i = pl.program_id(2)

    q = q_ref[...]
    lane = lax.broadcasted_iota(jnp.int32, q.shape, 1)
    zero = jnp.zeros_like(q)
    qs = jnp.concatenate([jnp.where(lane < head_dim, q, zero),
                          jnp.where(lane >= head_dim, q, zero)], axis=0)

    m_ref[...] = jnp.full(m_ref.shape, -jnp.inf, f32)
    l_ref[...] = jnp.zeros(l_ref.shape, f32)
    acc_ref[...] = jnp.zeros(acc_ref.shape, f32)

    def scores(j):
        start = pl.multiple_of(j * tk, tk)
        return lax.dot_general(qs, k_ref[pl.ds(start, tk), :], (((1,), (1,)), ((), ())),
                               preferred_element_type=f32)

    def softmax_pv(s, j, diag_sub):
        start = pl.multiple_of(j * tk, tk)
        vj = v_ref[pl.ds(start, tk), :]
        if diag_sub is not None:
            r = lax.broadcasted_iota(jnp.int32, (tq, tk), 0)
            c = lax.broadcasted_iota(jnp.int32, (tq, tk), 1) + diag_sub * tk
            keep = c <= r
            keep = jnp.concatenate([keep, keep], axis=0)
            s = jnp.where(keep, s, -jnp.inf)
        mx = s[:, :LANES]
        for c0 in range(LANES, tk, LANES):
            mx = jnp.maximum(mx, s[:, c0:c0 + LANES])
        m_prev = m_ref[...]
        m_new = jnp.maximum(m_prev, jnp.max(mx, axis=-1, keepdims=True))
        alpha = jnp.exp2(m_prev - m_new)
        ps = [jnp.exp2(s[:, c0:c0 + LANES] - m_new) for c0 in range(0, tk, LANES)]
        l_ref[...] = alpha * l_ref[...] + functools.reduce(lambda a, b: a + b, ps)
        p = jnp.concatenate([x.astype(bf16) for x in ps], axis=1)
        acc_ref[...] = alpha * acc_ref[...] + jnp.dot(p, vj, preferred_element_type=f32)
        m_ref[...] = m_new

    n_full = i * nsub

    def body(j, carry):
        softmax_pv(scores(j), j, None)
        return carry

    lax.fori_loop(0, n_full, body, 0)
    for d in range(nsub):
        softmax_pv(scores(n_full + d), n_full + d, d)

    lam = (jnp.exp(jnp.sum(lq1_ref[...] * lk1_ref[...], axis=-1, keepdims=True))
           - jnp.exp(jnp.sum(lq2_ref[...] * lk2_ref[...], axis=-1, keepdims=True)) + lam_init)
    o_all = acc_ref[...] / jnp.sum(l_ref[...], axis=-1, keepdims=True)
    o = o_all[:tq] - lam * o_all[tq:]
    o = _rms_rows(o) * sg_ref[...] * (1.0 - lam_init)
    o_ref[...] = o.astype(bf16)


def _attention(q, k, v, lq1, lk1, lq2, lk2, sg, B, S, H, head_dim, lam_init):
    T = q.shape[0]
    tq = min(512, S)
    tk = min(256, tq)
    nq = S // tq
    vd = sg.shape[1]
    return pl.pallas_call(
        functools.partial(_attn_kernel, lam_init=lam_init, head_dim=head_dim, tk=tk),
        grid=(B, H, nq),
        in_specs=[pl.BlockSpec((tq, 2 * head_dim), lambda b, h, i: (b * nq + i, h)),
                  pl.BlockSpec((S, 2 * head_dim), lambda b, h, i: (b, h)),
                  pl.BlockSpec((S, vd), lambda b, h, i: (b, h)),
                  _const_spec((1, head_dim)), _const_spec((1, head_dim)),
                  _const_spec((1, head_dim)), _const_spec((1, head_dim)),
                  _const_spec((1, vd))],
        out_specs=pl.BlockSpec((tq, vd), lambda b, h, i: (b * nq + i, h)),
        out_shape=jax.ShapeDtypeStruct((T, H * vd), bf16),
        scratch_shapes=[pltpu.VMEM((2 * tq, LANES), f32), pltpu.VMEM((2 * tq, LANES), f32),
                        pltpu.VMEM((2 * tq, vd), f32)],
        compiler_params=_params(("arbitrary", "arbitrary", "arbitrary")),
        name="diff_attn",
    )(q, k, v, lq1, lk1, lq2, lk2, sg)


def _post_kernel(ao_ref, gy_ref, ga_ref, x_ref, mod_ref, wao_ref, wo_ref, g2_ref, wr_ref, br_ref,
                 x1_ref, h2_ref, cw_ref):
    ya = jnp.dot(ao_ref[...], wao_ref[...], preferred_element_type=f32)
    mix_in = gy_ref[...].astype(f32) + ga_ref[...].astype(f32) * ya
    mix = jnp.dot(mix_in.astype(bf16), wo_ref[...], preferred_element_type=f32)
    gate1 = mod_ref[0, 2:3, :]
    x1 = x_ref[...] + gate1 * mix
    x1_ref[...] = x1
    h2 = _mod_h(x1, mod_ref, g2_ref, 3, 4).astype(bf16)
    h2_ref[...] = h2

    logits = jnp.dot(h2, wr_ref[...], preferred_element_type=f32) + br_ref[...]
    lane = lax.broadcasted_iota(jnp.int32, logits.shape, 1)
    work = logits
    sel, vals = [], []
    for _ in range(TOP_K):
        mx = jnp.max(work, axis=-1, keepdims=True)
        ek = jnp.min(jnp.where(work == mx, lane, LANES), axis=-1, keepdims=True)
        sel.append(ek)
        vals.append(mx)
        work = jnp.where(lane == ek, NEG_BIG, work)
    ex = [jnp.exp(vk - vals[0]) for vk in vals]
    inv_den = 1.0 / functools.reduce(lambda a, b: a + b, ex)
    cw = jnp.zeros(logits.shape, f32)
    for ek, xk in zip(sel, ex):
        cw = jnp.where(lane == ek, xk * inv_den, cw)
    cw_ref[...] = cw


def _post(ao, gy, ga, x2, mod3, wao, wo, g2, wr, br, S):
    T, D = x2.shape
    vw = ao.shape[1]
    tm = min(512, S)
    per_b = S // tm
    row = lambda t: (t, 0)
    return pl.pallas_call(
        _post_kernel,
        grid=(T // tm,),
        in_specs=[pl.BlockSpec((tm, vw), row), pl.BlockSpec((tm, D), row),
                  pl.BlockSpec((tm, D), row), pl.BlockSpec((tm, D), row),
                  pl.BlockSpec((1, 6, D), lambda t: (t // per_b, 0, 0)),
                  _const_spec((vw, D)), _const_spec((D, D)), _const_spec((1, D)),
                  _const_spec((D, LANES)), _const_spec((1, LANES))],
        out_specs=[pl.BlockSpec((tm, D), row), pl.BlockSpec((tm, D), row),
                   pl.BlockSpec((tm, LANES), row)],
        out_shape=[jax.ShapeDtypeStruct((T, D), f32), jax.ShapeDtypeStruct((T, D), bf16),
                   jax.ShapeDtypeStruct((T, LANES), f32)],
        compiler_params=_params(("arbitrary",)),
        name="post_router",
    )(ao, gy, ga, x2, mod3, wao, wo, g2, wr, br)


def _split_up_kernel(w_ref, perm_ref, g_ref, l_ref):
    n2 = w_ref.shape[2]
    perm = perm_ref[...]
    for b in range(n2 // (2 * LANES)):
        blk = w_ref[0, :, b * 2 * LANES:(b + 1) * 2 * LANES].astype(bf16)
        r = jnp.dot(blk, perm, preferred_element_type=f32)
        g_ref[0, :, b * LANES:(b + 1) * LANES] = r[:, :LANES].astype(bf16)
        l_ref[0, :, b * LANES:(b + 1) * LANES] = r[:, LANES:].astype(bf16)


def _split_up(w_up):
    E, D, F2 = w_up.shape
    F = F2 // 2
    src = np.arange(2 * LANES)
    dst = np.where(src % 2 == 0, src // 2, LANES + src // 2)
    perm = np.zeros((2 * LANES, 2 * LANES), np.float32)
    perm[src, dst] = 1.0
    ex = lambda e: (e, 0, 0)
    return pl.pallas_call(
        _split_up_kernel,
        grid=(E,),
        in_specs=[pl.BlockSpec((1, D, F2), ex), _const_spec((2 * LANES, 2 * LANES))],
        out_specs=[pl.BlockSpec((1, D, F), ex), pl.BlockSpec((1, D, F), ex)],
        out_shape=[jax.ShapeDtypeStruct((E, D, F), bf16), jax.ShapeDtypeStruct((E, D, F), bf16)],
        compiler_params=_params(("arbitrary",)),
        name="split_up",
    )(w_up, jnp.asarray(perm, bf16))


def _moe_dense_kernel(h2_ref, cw_ref, x1_ref, mod_ref, wg_ref, wl_ref, bgl_ref, bll_ref,
                      wd_ref, bd_ref, o_ref, acc_ref):
    e = pl.program_id(1)

    @pl.when(e == 0)
    def _():
        acc_ref[...] = jnp.zeros(acc_ref.shape, f32)

    h2 = h2_ref[...]
    glu = jnp.minimum(jnp.dot(h2, wg_ref[0], preferred_element_type=f32) + bgl_ref[0], SWIGLU_LIMIT)
    lin = jnp.clip(jnp.dot(h2, wl_ref[0], preferred_element_type=f32) + bll_ref[0],
                   -SWIGLU_LIMIT, SWIGLU_LIMIT)
    act = glu * jax.nn.sigmoid(SWIGLU_ALPHA * glu) * (lin + 1.0)
    y = jnp.dot(act.astype(bf16), wd_ref[0], preferred_element_type=f32) + bd_ref[0]
    cw = cw_ref[...]
    lane = lax.broadcasted_iota(jnp.int32, cw.shape, 1)
    ce = jnp.sum(jnp.where(lane == e, cw, 0.0), axis=-1, keepdims=True)
    acc_ref[...] += ce * y

    @pl.when(e == pl.num_programs(1) - 1)
    def _():
        o_ref[...] = x1_ref[...] + mod_ref[0, 5:6, :] * acc_ref[...]


def _moe_dense(h2, cw, x1, mod3, wg, wl, bgl, bll, wd, bdn, S):
    T, D = x1.shape
    E, _, F = wg.shape
    tm = min(512, S)
    per_b = S // tm
    row = lambda t, e: (t, 0)
    ex = lambda t, e: (e, 0, 0)
    return pl.pallas_call(
        _moe_dense_kernel,
        grid=(T // tm, E),
        in_specs=[pl.BlockSpec((tm, D), row), pl.BlockSpec((tm, LANES), row),
                  pl.BlockSpec((tm, D), row),
                  pl.BlockSpec((1, 6, D), lambda t, e: (t // per_b, 0, 0)),
                  pl.BlockSpec((1, D, F), ex), pl.BlockSpec((1, D, F), ex),
                  pl.BlockSpec((1, 1, F), ex), pl.BlockSpec((1, 1, F), ex),
                  pl.BlockSpec((1, F, D), ex), pl.BlockSpec((1, 1, D), ex)],
        out_specs=pl.BlockSpec((tm, D), row),
        out_shape=jax.ShapeDtypeStruct((T, D), f32),
        scratch_shapes=[pltpu.VMEM((tm, D), f32)],
        compiler_params=_params(("arbitrary", "arbitrary")),
        name="moe_dense",
    )(h2, cw, x1, mod3, wg, wl, bgl, bll, wd, bdn)


def _rope_lane_tables(head_dim):
    rot = head_dim // 4
    half = rot // 2
    lane = np.arange(LANES) % head_dim
    inv = np.where(lane < rot, ROPE_THETA ** (-(2.0 * (lane % half)) / rot), 0.0)
    m1 = np.where(lane < half, -1.0, 0.0)
    m2 = np.where((lane >= half) & (lane < rot), 1.0, 0.0)
    blk = np.arange(LANES) // head_dim
    bd = (blk[:, None] == blk[None, :]).astype(np.float32)
    as_row = lambda a: jnp.asarray(a, f32).reshape(1, LANES)
    return as_row(inv), as_row(m1), as_row(m2), jnp.asarray(bd, bf16)


def kernel(x, c, positions, w_ada, b_ada, norm1_g, w_in, conv_w, conv_b, conv_ln_g, conv_ln_b,
           w_conv_out, b_conv_out, q_norm_g, k_norm_g, lambda_q1, lambda_k1, lambda_q2, lambda_k2,
           subln_g, w_attn_out, b_gate, w_o, norm2_g, w_router, b_router, w_up, b_up, w_down, b_down):
    B, S, D = x.shape
    depth = w_ada.shape[0]
    hd = q_norm_g.shape[-1]
    vd = subln_g.shape[-1]
    vw = w_attn_out.shape[1]
    H = vw // vd
    qkw = H * 2 * hd
    E = w_router.shape[-1]
    F = w_down.shape[2]
    dc = conv_w.shape[-1]
    assert 2 * hd == LANES and vd == LANES and dc == D and E <= LANES
    T = B * S

    inv, m1, m2, bd = _rope_lane_tables(hd)
    pos = positions.astype(f32).reshape(T, 1)
    c_in = c
    x2 = x.reshape(T, D)
    for l in range(depth):
        lam_init = 0.8 - 0.6 * math.exp(-0.3 * l)
        mod3 = _ada(c_in, w_ada[l], b_ada[l]).reshape(B, 6, D)
        row = lambda a: a.reshape(1, -1)
        wi = w_in[l].astype(bf16)
        c0 = 2 * dc
        wa, wgt = wi[:, :dc], wi[:, dc:c0]
        wq, wk = wi[:, c0:c0 + qkw], wi[:, c0 + qkw:c0 + 2 * qkw]
        wv = wi[:, c0 + 2 * qkw:c0 + 2 * qkw + vw]
        wg = wi[:, c0 + 2 * qkw + vw:]
        g1 = row(norm1_g[l])

        gy, ga = _conv_branch(x2, mod3, g1, wa, wgt, wg, row(b_gate[l]), conv_w[l], row(conv_b[l]),
                              row(conv_ln_g[l]), row(conv_ln_b[l]), w_conv_out[l].astype(bf16),
                              row(b_conv_out[l]), B, S)
        qg = row(jnp.tile(q_norm_g[l], 2) * (hd ** -0.5 * math.log2(math.e)))
        kg = row(jnp.tile(k_norm_g[l], 2))
        q, k, v = _qkv(x2, mod3, g1, pos, wq, wk, wv, qg, kg, inv, m1, m2, bd, B, S, hd)
        ao = _attention(q, k, v, row(lambda_q1[l]), row(lambda_k1[l]), row(lambda_q2[l]),
                        row(lambda_k2[l]), row(subln_g[l]), B, S, H, hd, lam_init)

        wr = jnp.zeros((D, LANES), f32).at[:, :E].set(w_router[l]).astype(bf16)
        br = jnp.full((1, LANES), NEG_BIG, f32).at[0, :E].set(b_router[l])
        x1, h2, cw = _post(ao, gy, ga, x2, mod3, w_attn_out[l].astype(bf16), w_o[l].astype(bf16),
                           row(norm2_g[l]), wr, br, S)

        wu_g, wu_l = _split_up(w_up[l])
        x2 = _moe_dense(h2, cw, x1, mod3, wu_g, wu_l,
                        b_up[l][:, None, 0::2], b_up[l][:, None, 1::2],
                        w_down[l].astype(bf16), b_down[l][:, None, :], S)
    return x2.reshape(B, S, D)
```

```python
import functools
import math

import numpy as np
import jax
import jax.numpy as jnp
from jax import lax
from jax.experimental import pallas as pl
from jax.experimental.pallas import tpu as pltpu

EPS = 1e-6
ROPE_THETA = 500000.0
TOP_K = 4
SWIGLU_ALPHA = 1.702
SWIGLU_LIMIT = 7.0
LANES = 128
CONV_HALO = 32
NEG_BIG = -1e30
VMEM_LIMIT = 56 * 1024 * 1024

f32 = jnp.float32
bf16 = jnp.bfloat16


def _const_spec(shape):
    zeros = (0,) * len(shape)
    return pl.BlockSpec(shape, lambda *_: zeros, pipeline_mode=pl.Buffered(1))


def _params(sem):
    return pltpu.CompilerParams(dimension_semantics=sem, vmem_limit_bytes=VMEM_LIMIT)


def _rms_rows(x):
    return x * lax.rsqrt(jnp.mean(x * x, axis=-1, keepdims=True) + EPS)


def _mod_h(x, mod_ref, g_ref, shift_i, scale_i):
    shift = mod_ref[0, shift_i:shift_i + 1, :]
    scale = mod_ref[0, scale_i:scale_i + 1, :]
    return _rms_rows(x) * g_ref[...] * (1.0 + scale) + shift


def _ada_kernel(c_ref, w_ref, b_ref, o_ref):
    c = c_ref[...]
    ca = c * jax.nn.sigmoid(c)
    o_ref[...] = jnp.dot(ca.astype(bf16), w_ref[...].astype(bf16),
                         preferred_element_type=f32) + b_ref[...]


def _ada(c, w_ada, b_ada):
    B, D = c.shape
    N = w_ada.shape[1]
    bn = D
    assert N % bn == 0
    return pl.pallas_call(
        _ada_kernel,
        grid=(N // bn,),
        in_specs=[pl.BlockSpec((B, D), lambda j: (0, 0)),
                  pl.BlockSpec((D, bn), lambda j: (0, j)),
                  pl.BlockSpec((1, bn), lambda j: (0, j))],
        out_specs=pl.BlockSpec((B, bn), lambda j: (0, j)),
        out_shape=jax.ShapeDtypeStruct((B, N), f32),
        compiler_params=_params(("arbitrary",)),
        name="ada",
    )(c, w_ada, b_ada.reshape(1, N))


def _conv_kernel(x_ref, mod_ref, g1_ref, wa_ref, wgt_ref, wg_ref, bg_ref, cw_ref, cb_ref,
                 lng_ref, lnb_ref, wco_ref, bco_ref, gy_ref, ga_ref, ebuf, cbuf, *, width):
    ts, d = x_ref.shape
    i = pl.program_id(1)

    @pl.when(i == 0)
    def _():
        ebuf[0:CONV_HALO, :] = jnp.zeros((CONV_HALO, d), f32)

    hb = _mod_h(x_ref[...], mod_ref, g1_ref, 0, 1).astype(bf16)
    a = jnp.dot(hb, wa_ref[...], preferred_element_type=f32)
    gt = jnp.dot(hb, wgt_ref[...], preferred_element_type=f32)
    ebuf[CONV_HALO:CONV_HALO + ts, :] = a * jax.nn.sigmoid(gt)

    rc = min(32, ts)
    cc = min(512, d)
    base = CONV_HALO - (width - 1)
    for r0 in range(0, ts, rc):
        for c0 in range(0, d, cc):
            acc = jnp.broadcast_to(cb_ref[:, c0:c0 + cc], (rc, cc))
            for j in range(width):
                acc = acc + cw_ref[j:j + 1, c0:c0 + cc] * ebuf[base + j + r0:base + j + r0 + rc, c0:c0 + cc]
            cbuf[r0:r0 + rc, c0:c0 + cc] = acc
    ebuf[0:CONV_HALO, :] = ebuf[ts:ts + CONV_HALO, :]

    cv = cbuf[...]
    mu = jnp.mean(cv, axis=-1, keepdims=True)
    xc = cv - mu
    var = jnp.mean(xc * xc, axis=-1, keepdims=True)
    ln = xc * lax.rsqrt(var + EPS) * lng_ref[...] + lnb_ref[...]
    sw = ln * jax.nn.sigmoid(ln)
    y_conv = jnp.dot(sw.astype(bf16), wco_ref[...], preferred_element_type=f32) + bco_ref[...]

    g = jax.nn.sigmoid(jnp.dot(hb, wg_ref[...], preferred_element_type=f32) + bg_ref[...])
    gy_ref[...] = (g[:, :d] * y_conv).astype(bf16)
    ga_ref[...] = g[:, d:].astype(bf16)


def _conv_branch(x2, mod3, g1, wa, wgt, wg, bg, cw, cb, lng, lnb, wco, bco, B, S):
    T, D = x2.shape
    width = cw.shape[0]
    ts = min(256, S)
    ns = S // ts
    row = lambda b, i: (b * ns + i, 0)
    return pl.pallas_call(
        functools.partial(_conv_kernel, width=width),
        grid=(B, ns),
        in_specs=[pl.BlockSpec((ts, D), row),
                  pl.BlockSpec((1, 6, D), lambda b, i: (b, 0, 0)),
                  _const_spec((1, D)), _const_spec((D, D)), _const_spec((D, D)),
                  _const_spec((D, 2 * D)), _const_spec((1, 2 * D)),
                  _const_spec((width, D)), _const_spec((1, D)), _const_spec((1, D)),
                  _const_spec((1, D)), _const_spec((D, D)), _const_spec((1, D))],
        out_specs=[pl.BlockSpec((ts, D), row), pl.BlockSpec((ts, D), row)],
        out_shape=[jax.ShapeDtypeStruct((T, D), bf16), jax.ShapeDtypeStruct((T, D), bf16)],
        scratch_shapes=[pltpu.VMEM((CONV_HALO + ts, D), f32), pltpu.VMEM((ts, D), f32)],
        compiler_params=_params(("arbitrary", "arbitrary")),
        name="conv_branch",
    )(x2, mod3, g1, wa, wgt, wg, bg, cw, cb, lng, lnb, wco, bco)


def _qkv_kernel(x_ref, mod_ref, g1_ref, pos_ref, wq_ref, wk_ref, wv_ref, qg_ref, kg_ref,
                inv_ref, m1_ref, m2_ref, bd_ref, q_ref, k_ref, v_ref, *, head_dim):
    hb = _mod_h(x_ref[...], mod_ref, g1_ref, 0, 1).astype(bf16)
    v_ref[...] = jnp.dot(hb, wv_ref[...], preferred_element_type=f32).astype(bf16)

    ang = pos_ref[...] * inv_ref[...]
    cosv = jnp.cos(ang)
    sinv = jnp.sin(ang)
    s_lo = sinv * m1_ref[...]
    s_hi = sinv * m2_ref[...]
    half = head_dim // 8
    bd = bd_ref[...]
    inv_hd = 1.0 / head_dim

    def norm_rope(w_ref, g_ref, o_ref):
        t = jnp.dot(hb, w_ref[...], preferred_element_type=f32)
        for c0 in range(0, t.shape[1], LANES):
            tb = t[:, c0:c0 + LANES]
            ss = jnp.dot((tb * tb).astype(bf16), bd, preferred_element_type=f32)
            tn = tb * lax.rsqrt(ss * inv_hd + EPS) * g_ref[...]
            rot = (tn * cosv + pltpu.roll(tn, LANES - half, 1) * s_lo
                   + pltpu.roll(tn, half, 1) * s_hi)
            o_ref[:, c0:c0 + LANES] = rot.astype(bf16)

    norm_rope(wq_ref, qg_ref, q_ref)
    norm_rope(wk_ref, kg_ref, k_ref)


def _qkv(x2, mod3, g1, pos, wq, wk, wv, qg, kg, inv, m1, m2, bd, B, S, head_dim):
    T, D = x2.shape
    qkw, vw = wq.shape[1], wv.shape[1]
    ts = min(512, S)
    ns = S // ts
    row = lambda b, i: (b * ns + i, 0)
    return pl.pallas_call(
        functools.partial(_qkv_kernel, head_dim=head_dim),
        grid=(B, ns),
        in_specs=[pl.BlockSpec((ts, D), row),
                  pl.BlockSpec((1, 6, D), lambda b, i: (b, 0, 0)),
                  _const_spec((1, D)),
                  pl.BlockSpec((ts, 1), row),
                  _const_spec((D, qkw)), _const_spec((D, qkw)), _const_spec((D, vw)),
                  _const_spec((1, LANES)), _const_spec((1, LANES)), _const_spec((1, LANES)),
                  _const_spec((1, LANES)), _const_spec((1, LANES)), _const_spec((LANES, LANES))],
        out_specs=[pl.BlockSpec((ts, qkw), row), pl.BlockSpec((ts, qkw), row),
                   pl.BlockSpec((ts, vw), row)],
        out_shape=[jax.ShapeDtypeStruct((T, qkw), bf16), jax.ShapeDtypeStruct((T, qkw), bf16),
                   jax.ShapeDtypeStruct((T, vw), bf16)],
        compiler_params=_params(("arbitrary", "arbitrary")),
        name="qkv",
    )(x2, mod3, g1, pos, wq, wk, wv, qg, kg, inv, m1, m2, bd)


def _attn_kernel(q_ref, k_ref, v_ref, lq1_ref, lk1_ref, lq2_ref, lk2_ref, sg_ref, o_ref,
                 m_ref, l_ref, acc_ref, *, lam_init, head_dim, tk):
    tq = q_ref.shape[0]
    nsub = tq // tk
    i = pl.program_id(2)

    q = q_ref[...]
    lane = lax.broadcasted_iota(jnp.int32, q.shape, 1)
    zero = jnp.zeros_like(q)
    qs = jnp.concatenate([jnp.where(lane < head_dim, q, zero),
                          jnp.where(lane >= head_dim, q, zero)], axis=0)

    m_ref[...] = jnp.full(m_ref.shape, -jnp.inf, f32)
    l_ref[...] = jnp.zeros(l_ref.shape, f32)
    acc_ref[...] = jnp.zeros(acc_ref.shape, f32)

    def scores(j):
        start = pl.multiple_of(j * tk, tk)
        return lax.dot_general(qs, k_ref[pl.ds(start, tk), :], (((1,), (1,)), ((), ())),
                               preferred_element_type=f32)

    def softmax_pv(s, j, diag_sub):
        start = pl.multiple_of(j * tk, tk)
        vj = v_ref[pl.ds(start, tk), :]
        if diag_sub is not None:
            r = lax.broadcasted_iota(jnp.int32, (tq, tk), 0)
            c = lax.broadcasted_iota(jnp.int32, (tq, tk), 1) + diag_sub * tk
            keep = c <= r
            keep = jnp.concatenate([keep, keep], axis=0)
            s = jnp.where(keep, s, -jnp.inf)
        mx = s[:, :LANES]
        for c0 in range(LANES, tk, LANES):
            mx = jnp.maximum(mx, s[:, c0:c0 + LANES])
        m_prev = m_ref[...]
        m_new = jnp.maximum(m_prev, jnp.max(mx, axis=-1, keepdims=True))
        alpha = jnp.exp2(m_prev - m_new)
        ps = [jnp.exp2(s[:, c0:c0 + LANES] - m_new) for c0 in range(0, tk, LANES)]
        l_ref[...] = alpha * l_ref[...] + functools.reduce(lambda a, b: a + b, ps)
        p = jnp.concatenate([x.astype(bf16) for x in ps], axis=1)
        acc_ref[...] = alpha * acc_ref[...] + jnp.dot(p, vj, preferred_element_type=f32)
        m_ref[...] = m_new

    n_full = i * nsub

    def body(j, carry):
        softmax_pv(scores(j), j, None)
        return carry

    lax.fori_loop(0, n_full, body, 0)
    for d in range(nsub):
        softmax_pv(scores(n_full + d), n_full + d, d)

    lam = (jnp.exp(jnp.sum(lq1_ref[...] * lk1_ref[...], axis=-1, keepdims=True))
           - jnp.exp(jnp.sum(lq2_ref[...] * lk2_ref[...], axis=-1, keepdims=True)) + lam_init)
    o_all = acc_ref[...] / jnp.sum(l_ref[...], axis=-1, keepdims=True)
    o = o_all[:tq] - lam * o_all[tq:]
    o = _rms_rows(o) * sg_ref[...] * (1.0 - lam_init)
    o_ref[...] = o.astype(bf16)


def _attention(q, k, v, lq1, lk1, lq2, lk2, sg, B, S, H, head_dim, lam_init):
    T = q.shape[0]
    tq = min(512, S)
    tk = min(256, tq)
    nq = S // tq
    vd = sg.shape[1]
    return pl.pallas_call(
        functools.partial(_attn_kernel, lam_init=lam_init, head_dim=head_dim, tk=tk),
        grid=(B, H, nq),
        in_specs=[pl.BlockSpec((tq, 2 * head_dim), lambda b, h, i: (b * nq + i, h)),
                  pl.BlockSpec((S, 2 * head_dim), lambda b, h, i: (b, h)),
                  pl.BlockSpec((S, vd), lambda b, h, i: (b, h)),
                  _const_spec((1, head_dim)), _const_spec((1, head_dim)),
                  _const_spec((1, head_dim)), _const_spec((1, head_dim)),
                  _const_spec((1, vd))],
        out_specs=pl.BlockSpec((tq, vd), lambda b, h, i: (b * nq + i, h)),
        out_shape=jax.ShapeDtypeStruct((T, H * vd), bf16),
        scratch_shapes=[pltpu.VMEM((2 * tq, LANES), f32), pltpu.VMEM((2 * tq, LANES), f32),
                        pltpu.VMEM((2 * tq, vd), f32)],
        compiler_params=_params(("arbitrary", "arbitrary", "arbitrary")),
        name="diff_attn",
    )(q, k, v, lq1, lk1, lq2, lk2, sg)


def _pack_bf16_pair(lo, hi):
    lo_bits = pltpu.bitcast(lo.astype(bf16).astype(f32), jnp.uint32)
    hi_bits = pltpu.bitcast(hi.astype(bf16).astype(f32), jnp.uint32)
    return (lo_bits >> 16) | (hi_bits & jnp.uint32(0xFFFF0000))


def _unpack_bf16_pair(w):
    lo = pltpu.bitcast(w << 16, f32)
    hi = pltpu.bitcast(w & jnp.uint32(0xFFFF0000), f32)
    return lo, hi


def _post_kernel(ao_ref, gy_ref, ga_ref, x_ref, mod_ref, wao_ref, wo_ref, g2_ref, wr_ref, br_ref,
                 x1_ref, h2p_ref, meta_ref, cnt_ref, run_ref):
    tm, d = x_ref.shape

    @pl.when(pl.program_id(0) == 0)
    def _():
        run_ref[...] = jnp.zeros(run_ref.shape, f32)

    ya = jnp.dot(ao_ref[...], wao_ref[...], preferred_element_type=f32)
    mix_in = gy_ref[...].astype(f32) + ga_ref[...].astype(f32) * ya
    mix = jnp.dot(mix_in.astype(bf16), wo_ref[...], preferred_element_type=f32)
    gate1 = mod_ref[0, 2:3, :]
    x1 = x_ref[...] + gate1 * mix
    x1_ref[...] = x1
    h2f = _mod_h(x1, mod_ref, g2_ref, 3, 4)
    h2p_ref[...] = _pack_bf16_pair(h2f[:, :d // 2], h2f[:, d // 2:])

    logits = jnp.dot(h2f.astype(bf16), wr_ref[...], preferred_element_type=f32) + br_ref[...]
    lane = lax.broadcasted_iota(jnp.int32, logits.shape, 1)
    work = logits
    sel, vals = [], []
    for _ in range(TOP_K):
        mx = jnp.max(work, axis=-1, keepdims=True)
        ek = jnp.min(jnp.where(work == mx, lane, LANES), axis=-1, keepdims=True)
        sel.append(ek)
        vals.append(mx)
        work = jnp.where(lane == ek, NEG_BIG, work)
    ex = [jnp.exp(vk - vals[0]) for vk in vals]
    inv_den = 1.0 / functools.reduce(lambda a, b: a + b, ex)

    mask = jnp.zeros(logits.shape, f32)
    for ek in sel:
        mask = jnp.where(lane == ek, 1.0, mask)
    r_i = lax.broadcasted_iota(jnp.int32, (tm, tm), 0)
    c_i = lax.broadcasted_iota(jnp.int32, (tm, tm), 1)
    earlier = jnp.where(c_i < r_i, 1.0, 0.0).astype(bf16)
    rank = jnp.dot(earlier, mask.astype(bf16), preferred_element_type=f32) + run_ref[...]
    run_ref[...] = run_ref[...] + jnp.sum(mask, axis=0, keepdims=True)
    cnt_ref[...] = run_ref[...]

    meta = jnp.zeros(logits.shape, f32)
    for k, (ek, xk) in enumerate(zip(sel, ex)):
        rk = jnp.sum(jnp.where(lane == ek, rank, 0.0), axis=-1, keepdims=True)
        meta = jnp.where(lane == k, ek.astype(f32), meta)
        meta = jnp.where(lane == TOP_K + k, rk, meta)
        meta = jnp.where(lane == 2 * TOP_K + k, xk * inv_den, meta)
    meta_ref[...] = meta


def _post(ao, gy, ga, x2, mod3, wao, wo, g2, wr, br, S):
    T, D = x2.shape
    vw = ao.shape[1]
    tm = min(512, S)
    per_b = S // tm
    row = lambda t: (t, 0)
    return pl.pallas_call(
        _post_kernel,
        grid=(T // tm,),
        in_specs=[pl.BlockSpec((tm, vw), row), pl.BlockSpec((tm, D), row),
                  pl.BlockSpec((tm, D), row), pl.BlockSpec((tm, D), row),
                  pl.BlockSpec((1, 6, D), lambda t: (t // per_b, 0, 0)),
                  _const_spec((vw, D)), _const_spec((D, D)), _const_spec((1, D)),
                  _const_spec((D, LANES)), _const_spec((1, LANES))],
        out_specs=[pl.BlockSpec((tm, D), row), pl.BlockSpec((tm, D // 2), row),
                   pl.BlockSpec((tm, LANES), row), pl.BlockSpec((1, LANES), lambda t: (0, 0))],
        out_shape=[jax.ShapeDtypeStruct((T, D), f32), jax.ShapeDtypeStruct((T, D // 2), jnp.uint32),
                   jax.ShapeDtypeStruct((T, LANES), f32), jax.ShapeDtypeStruct((1, LANES), f32)],
        scratch_shapes=[pltpu.VMEM((1, LANES), f32)],
        compiler_params=_params(("arbitrary",)),
        name="post_router",
    )(ao, gy, ga, x2, mod3, wao, wo, g2, wr, br)


def _split_up_kernel(w_ref, perm_ref, g_ref, l_ref):
    n2 = w_ref.shape[2]
    perm = perm_ref[...]
    for b in range(n2 // (2 * LANES)):
        blk = w_ref[0, :, b * 2 * LANES:(b + 1) * 2 * LANES].astype(bf16)
        r = jnp.dot(blk, perm, preferred_element_type=f32)
        g_ref[0, :, b * LANES:(b + 1) * LANES] = r[:, :LANES].astype(bf16)
        l_ref[0, :, b * LANES:(b + 1) * LANES] = r[:, LANES:].astype(bf16)


def _split_up(w_up):
    E, D, F2 = w_up.shape
    F = F2 // 2
    src = np.arange(2 * LANES)
    dst = np.where(src % 2 == 0, src // 2, LANES + src // 2)
    perm = np.zeros((2 * LANES, 2 * LANES), np.float32)
    perm[src, dst] = 1.0
    ex = lambda e: (e, 0, 0)
    return pl.pallas_call(
        _split_up_kernel,
        grid=(E,),
        in_specs=[pl.BlockSpec((1, D, F2), ex), _const_spec((2 * LANES, 2 * LANES))],
        out_specs=[pl.BlockSpec((1, D, F), ex), pl.BlockSpec((1, D, F), ex)],
        out_shape=[jax.ShapeDtypeStruct((E, D, F), bf16), jax.ShapeDtypeStruct((E, D, F), bf16)],
        compiler_params=_params(("arbitrary",)),
        name="split_up",
    )(w_up, jnp.asarray(perm, bf16))


def _moe_kernel(te_ref, nu_ref, tokc_ref, tokn_ref, dstp_ref, h2p_hbm, wg_ref, wl_ref, bg_ref, bl_ref,
                wd_ref, bd_ref, y_hbm, xbuf, ybuf, gsem, ssem):
    tm = xbuf.shape[1]
    i = pl.program_id(0)
    nu = nu_ref[0]
    slot = i % 2
    other = 1 - slot

    def gather_row(tok_ref, s, r):
        pltpu.make_async_copy(h2p_hbm.at[tok_ref[0, 0, r]], xbuf.at[s, r], gsem.at[s]).start()

    def scatter_row(s, r):
        pltpu.make_async_copy(ybuf.at[s, r], y_hbm.at[dstp_ref[0, 0, r]], ssem.at[s]).start()

    def wait_gather(s):
        pltpu.make_async_copy(h2p_hbm.at[pl.ds(0, tm)], xbuf.at[s], gsem.at[s]).wait()

    def wait_scatter(s):
        pltpu.make_async_copy(ybuf.at[s], y_hbm.at[pl.ds(0, tm)], ssem.at[s]).wait()

    @pl.when(i == 0)
    def _():
        ybuf[...] = jnp.zeros(ybuf.shape, ybuf.dtype)

        def issue(r, carry):
            gather_row(tokc_ref, 0, r)
            return carry
        lax.fori_loop(0, tm, issue, 0, unroll=8)

    @pl.when(i < nu)
    def _():
        wait_gather(slot)

        @pl.when(i >= 1)
        def _():
            wait_scatter(slot)

        for r in range(tm):
            gather_row(tokn_ref, other, r)
            scatter_row(other, r)

        lo, hi = _unpack_bf16_pair(xbuf[slot])
        x = jnp.concatenate([lo, hi], axis=1).astype(bf16)
        glu = jnp.minimum(jnp.dot(x, wg_ref[0], preferred_element_type=f32) + bg_ref[0], SWIGLU_LIMIT)
        lin = jnp.clip(jnp.dot(x, wl_ref[0], preferred_element_type=f32) + bl_ref[0],
                       -SWIGLU_LIMIT, SWIGLU_LIMIT)
        act = glu * jax.nn.sigmoid(SWIGLU_ALPHA * glu) * (lin + 1.0)
        y = jnp.dot(act.astype(bf16), wd_ref[0], preferred_element_type=f32) + bd_ref[0]
        dh = y.shape[1] // 2
        ybuf[slot] = _pack_bf16_pair(y[:, :dh], y[:, dh:])

    @pl.when(i == nu)
    def _():
        def issue(r, carry):
            scatter_row(other, r)
            return carry
        lax.fori_loop(0, tm, issue, 0, unroll=8)
        wait_scatter(other)
        wait_scatter(slot)
        wait_gather(slot)


def _moe_routed(h2p, tile_e, n_used, tok_tab, dst_tab, wg, wl, bg, bl, wd, bdn, n_out_rows, tm):
    T, dh = h2p.shape
    E, D, F = wg.shape
    nt = tok_tab.shape[0]
    ex = lambda i, te, nu: (te[i], 0, 0)
    smem_blk = lambda f: pl.BlockSpec((1, 1, tm), f, memory_space=pltpu.SMEM)
    grid_spec = pltpu.PrefetchScalarGridSpec(
        num_scalar_prefetch=2,
        grid=(nt,),
        in_specs=[smem_blk(lambda i, te, nu: (i, 0, 0)),
                  smem_blk(lambda i, te, nu: (jnp.minimum(i + 1, nt - 1), 0, 0)),
                  smem_blk(lambda i, te, nu: (i, 0, 0)),
                  pl.BlockSpec(memory_space=pl.ANY),
                  pl.BlockSpec((1, D, F), ex), pl.BlockSpec((1, D, F), ex),
                  pl.BlockSpec((1, 1, F), ex), pl.BlockSpec((1, 1, F), ex),
                  pl.BlockSpec((1, F, D), ex), pl.BlockSpec((1, 1, D), ex)],
        out_specs=pl.BlockSpec(memory_space=pl.ANY),
        scratch_shapes=[pltpu.VMEM((2, tm, dh), jnp.uint32), pltpu.VMEM((2, tm, dh), jnp.uint32),
                        pltpu.SemaphoreType.DMA((2,)), pltpu.SemaphoreType.DMA((2,))],
    )
    return pl.pallas_call(
        _moe_kernel,
        grid_spec=grid_spec,
        out_shape=jax.ShapeDtypeStruct((n_out_rows, dh), jnp.uint32),
        compiler_params=_params(("arbitrary",)),
        name="moe_routed",
    )(tile_e, n_used, tok_tab, tok_tab, dst_tab, h2p, wg, wl, bg, bl, wd, bdn)


def _route_tables(meta, counts, E, tm):
    T = meta.shape[0]
    e = meta[:, 0:TOP_K].astype(jnp.int32)
    rk = meta[:, TOP_K:2 * TOP_K].astype(jnp.int32)
    cnt = counts[0, :E].astype(jnp.int32)
    ntile = (cnt + tm - 1) // tm
    tile_end = jnp.cumsum(ntile)
    slot = ((tile_end - ntile) * tm)[e] + rk
    P = T * TOP_K + E * tm
    nt = P // tm
    flat = slot.reshape(-1)
    pair = jnp.arange(T * TOP_K, dtype=jnp.int32)
    tok_tab = jnp.zeros((P,), jnp.int32).at[flat].set(pair // TOP_K, unique_indices=True)
    dst_tab = (T * TOP_K + jnp.arange(P + tm, dtype=jnp.int32) % tm).at[flat + tm].set(
        (pair % TOP_K) * T + pair // TOP_K, unique_indices=True)[:P]
    n_used = tile_end[-1:]
    tile_e = jnp.searchsorted(tile_end, jnp.arange(nt, dtype=jnp.int32), side='right')
    last_e = jnp.searchsorted(tile_end, n_used[0] - 1, side='right')
    tile_e = jnp.minimum(tile_e, last_e).astype(jnp.int32)
    return tile_e, n_used.astype(jnp.int32), tok_tab.reshape(nt, 1, tm), dst_tab.reshape(nt, 1, tm), P


def _combine_kernel(x1_ref, meta_ref, mod_ref, y0_ref, y1_ref, y2_ref, y3_ref, o_ref):
    dh = y0_ref.shape[1]
    meta = meta_ref[...]
    acc_lo = acc_hi = None
    for k, y_ref in enumerate((y0_ref, y1_ref, y2_ref, y3_ref)):
        w = meta[:, 2 * TOP_K + k:2 * TOP_K + k + 1]
        lo, hi = _unpack_bf16_pair(y_ref[...])
        acc_lo = w * lo if acc_lo is None else acc_lo + w * lo
        acc_hi = w * hi if acc_hi is None else acc_hi + w * hi
    gate2 = mod_ref[0, 5:6, :]
    o_ref[:, :dh] = x1_ref[:, :dh] + gate2[:, :dh] * acc_lo
    o_ref[:, dh:] = x1_ref[:, dh:] + gate2[:, dh:] * acc_hi


def _combine(x1, meta, mod3, y, S):
    T, D = x1.shape
    dh = D // 2
    tm = min(512, S)
    per_b = S // tm
    nb = T // tm
    row = lambda t: (t, 0)
    plane = lambda k: pl.BlockSpec((tm, dh), lambda t: (k * nb + t, 0))
    return pl.pallas_call(
        _combine_kernel,
        grid=(nb,),
        in_specs=[pl.BlockSpec((tm, D), row), pl.BlockSpec((tm, LANES), row),
                  pl.BlockSpec((1, 6, D), lambda t: (t // per_b, 0, 0)),
                  plane(0), plane(1), plane(2), plane(3)],
        out_specs=pl.BlockSpec((tm, D), row),
        out_shape=jax.ShapeDtypeStruct((T, D), f32),
        compiler_params=_params(("arbitrary",)),
        name="combine",
    )(x1, meta, mod3, y, y, y, y)


def _rope_lane_tables(head_dim):
    rot = head_dim // 4
    half = rot // 2
    lane = np.arange(LANES) % head_dim
    inv = np.where(lane < rot, ROPE_THETA ** (-(2.0 * (lane % half)) / rot), 0.0)
    m1 = np.where(lane < half, -1.0, 0.0)
    m2 = np.where((lane >= half) & (lane < rot), 1.0, 0.0)
    blk = np.arange(LANES) // head_dim
    bd = (blk[:, None] == blk[None, :]).astype(np.float32)
    as_row = lambda a: jnp.asarray(a, f32).reshape(1, LANES)
    return as_row(inv), as_row(m1), as_row(m2), jnp.asarray(bd, bf16)


def kernel(x, c, positions, w_ada, b_ada, norm1_g, w_in, conv_w, conv_b, conv_ln_g, conv_ln_b,
           w_conv_out, b_conv_out, q_norm_g, k_norm_g, lambda_q1, lambda_k1, lambda_q2, lambda_k2,
           subln_g, w_attn_out, b_gate, w_o, norm2_g, w_router, b_router, w_up, b_up, w_down, b_down):
    B, S, D = x.shape
    depth = w_ada.shape[0]
    hd = q_norm_g.shape[-1]
    vd = subln_g.shape[-1]
    vw = w_attn_out.shape[1]
    H = vw // vd
    qkw = H * 2 * hd
    E = w_router.shape[-1]
    F = w_down.shape[2]
    dc = conv_w.shape[-1]
    assert 2 * hd == LANES and vd == LANES and dc == D and E <= LANES
    T = B * S

    inv, m1, m2, bd = _rope_lane_tables(hd)
    pos = positions.astype(f32).reshape(T, 1)
    c_in = c
    x2 = x.reshape(T, D)
    for l in range(depth):
        lam_init = 0.8 - 0.6 * math.exp(-0.3 * l)
        mod3 = _ada(c_in, w_ada[l], b_ada[l]).reshape(B, 6, D)
        row = lambda a: a.reshape(1, -1)
        wi = w_in[l].astype(bf16)
        c0 = 2 * dc
        wa, wgt = wi[:, :dc], wi[:, dc:c0]
        wq, wk = wi[:, c0:c0 + qkw], wi[:, c0 + qkw:c0 + 2 * qkw]
        wv = wi[:, c0 + 2 * qkw:c0 + 2 * qkw + vw]
        wg = wi[:, c0 + 2 * qkw + vw:]
        g1 = row(norm1_g[l])

        gy, ga = _conv_branch(x2, mod3, g1, wa, wgt, wg, row(b_gate[l]), conv_w[l], row(conv_b[l]),
                              row(conv_ln_g[l]), row(conv_ln_b[l]), w_conv_out[l].astype(bf16),
                              row(b_conv_out[l]), B, S)
        qg = row(jnp.tile(q_norm_g[l], 2) * (hd ** -0.5 * math.log2(math.e)))
        kg = row(jnp.tile(k_norm_g[l], 2))
        q, k, v = _qkv(x2, mod3, g1, pos, wq, wk, wv, qg, kg, inv, m1, m2, bd, B, S, hd)
        ao = _attention(q, k, v, row(lambda_q1[l]), row(lambda_k1[l]), row(lambda_q2[l]),
                        row(lambda_k2[l]), row(subln_g[l]), B, S, H, hd, lam_init)

        wr = jnp.zeros((D, LANES), f32).at[:, :E].set(w_router[l]).astype(bf16)
        br = jnp.full((1, LANES), NEG_BIG, f32).at[0, :E].set(b_router[l])
        x1, h2p, meta, counts = _post(ao, gy, ga, x2, mod3, w_attn_out[l].astype(bf16),
                                      w_o[l].astype(bf16), row(norm2_g[l]), wr, br, S)

        tm_e = min(512, T)
        tile_e, n_used, tok_tab, dst_tab, P = _route_tables(meta, counts, E, tm_e)
        wu_g, wu_l = _split_up(w_up[l])
        y = _moe_routed(h2p, tile_e, n_used, tok_tab, dst_tab, wu_g, wu_l,
                        b_up[l][:, None, 0::2], b_up[l][:, None, 1::2],
                        w_down[l].astype(bf16), b_down[l][:, None, :], T * TOP_K + tm_e, tm_e)
        x2 = _combine(x1, meta, mod3, y, S)
    return x2.reshape(B, S, D)
```

```python
import functools
import math

import numpy as np
import jax
import jax.numpy as jnp
from jax import lax
from jax.experimental import pallas as pl
from jax.experimental.pallas import tpu as pltpu

EPS = 1e-6
ROPE_THETA = 500000.0
TOP_K = 4
SWIGLU_ALPHA = 1.702
SWIGLU_LIMIT = 7.0
LANES = 128
SUBLANES = 8
CONV_HALO = 32
NEG_BIG = -1e30
VMEM_LIMIT = 56 * 1024 * 1024

f32 = jnp.float32
bf16 = jnp.bfloat16


def _const_spec(shape):
    zeros = (0,) * len(shape)
    return pl.BlockSpec(shape, lambda *_: zeros, pipeline_mode=pl.Buffered(1))


def _params(sem):
    return pltpu.CompilerParams(dimension_semantics=sem, vmem_limit_bytes=VMEM_LIMIT)


def _rms_rows(x):
    return x * lax.rsqrt(jnp.mean(x * x, axis=-1, keepdims=True) + EPS)


def _mod_h(x, mod_ref, g_ref, shift_i, scale_i):
    shift = mod_ref[0, shift_i:shift_i + 1, :]
    scale = mod_ref[0, scale_i:scale_i + 1, :]
    return _rms_rows(x) * g_ref[...] * (1.0 + scale) + shift


def _ada_kernel(c_ref, w_ref, b_ref, o_ref):
    c = c_ref[...]
    ca = c * jax.nn.sigmoid(c)
    o_ref[...] = jnp.dot(ca.astype(bf16), w_ref[...].astype(bf16),
                         preferred_element_type=f32) + b_ref[...]


def _ada(c, w_ada, b_ada):
    B, D = c.shape
    N = w_ada.shape[1]
    bn = D
    assert N % bn == 0
    return pl.pallas_call(
        _ada_kernel,
        grid=(N // bn,),
        in_specs=[pl.BlockSpec((B, D), lambda j: (0, 0)),
                  pl.BlockSpec((D, bn), lambda j: (0, j)),
                  pl.BlockSpec((1, bn), lambda j: (0, j))],
        out_specs=pl.BlockSpec((B, bn), lambda j: (0, j)),
        out_shape=jax.ShapeDtypeStruct((B, N), f32),
        compiler_params=_params(("arbitrary",)),
        name="ada",
    )(c, w_ada, b_ada.reshape(1, N))


def _conv_kernel(x_ref, mod_ref, g1_ref, wa_ref, wgt_ref, wg_ref, bg_ref, cw_ref, cb_ref,
                 lng_ref, lnb_ref, wco_ref, bco_ref, gy_ref, ga_ref, ebuf, cbuf, sh_ref, *, width):
    ts, d = x_ref.shape
    i = pl.program_id(1)

    @pl.when(i == 0)
    def _():
        ebuf[0:CONV_HALO, :] = jnp.zeros((CONV_HALO, d), f32)

    hb = _mod_h(x_ref[...], mod_ref, g1_ref, 0, 1).astype(bf16)
    a = jnp.dot(hb, wa_ref[...], preferred_element_type=f32)
    gt = jnp.dot(hb, wgt_ref[...], preferred_element_type=f32)
    ebuf[CONV_HALO:CONV_HALO + ts, :] = a * jax.nn.sigmoid(gt)

    base = CONV_HALO - (width - 1)
    n_sh = ts + CONV_HALO - SUBLANES
    for r in range(1, SUBLANES):
        sh_ref[r - 1, 0:n_sh, :] = ebuf[r:r + n_sh, :]

    def tap(j, r0, rc, c0, cc):
        a, r = divmod(base + j, SUBLANES)
        lo = SUBLANES * a + r0
        if r == 0:
            return ebuf[lo:lo + rc, c0:c0 + cc]
        return sh_ref[r - 1, lo:lo + rc, c0:c0 + cc]

    rc = min(32, ts)
    cc = min(512, d)
    for r0 in range(0, ts, rc):
        for c0 in range(0, d, cc):
            acc = jnp.broadcast_to(cb_ref[:, c0:c0 + cc], (rc, cc))
            for j in range(width):
                acc = acc + cw_ref[j:j + 1, c0:c0 + cc] * tap(j, r0, rc, c0, cc)
            cbuf[r0:r0 + rc, c0:c0 + cc] = acc
    ebuf[0:CONV_HALO, :] = ebuf[ts:ts + CONV_HALO, :]

    cv = cbuf[...]
    mu = jnp.mean(cv, axis=-1, keepdims=True)
    xc = cv - mu
    var = jnp.mean(xc * xc, axis=-1, keepdims=True)
    ln = xc * lax.rsqrt(var + EPS) * lng_ref[...] + lnb_ref[...]
    sw = ln * jax.nn.sigmoid(ln)
    y_conv = jnp.dot(sw.astype(bf16), wco_ref[...], preferred_element_type=f32) + bco_ref[...]

    g = jax.nn.sigmoid(jnp.dot(hb, wg_ref[...], preferred_element_type=f32) + bg_ref[...])
    gy_ref[...] = (g[:, :d] * y_conv).astype(bf16)
    ga_ref[...] = g[:, d:].astype(bf16)


def _conv_branch(x2, mod3, g1, wa, wgt, wg, bg, cw, cb, lng, lnb, wco, bco, B, S):
    T, D = x2.shape
    width = cw.shape[0]
    ts = min(256, S)
    ns = S // ts
    row = lambda b, i: (b * ns + i, 0)
    return pl.pallas_call(
        functools.partial(_conv_kernel, width=width),
        grid=(B, ns),
        in_specs=[pl.BlockSpec((ts, D), row),
                  pl.BlockSpec((1, 6, D), lambda b, i: (b, 0, 0)),
                  _const_spec((1, D)), _const_spec((D, D)), _const_spec((D, D)),
                  _const_spec((D, 2 * D)), _const_spec((1, 2 * D)),
                  _const_spec((width, D)), _const_spec((1, D)), _const_spec((1, D)),
                  _const_spec((1, D)), _const_spec((D, D)), _const_spec((1, D))],
        out_specs=[pl.BlockSpec((ts, D), row), pl.BlockSpec((ts, D), row)],
        out_shape=[jax.ShapeDtypeStruct((T, D), bf16), jax.ShapeDtypeStruct((T, D), bf16)],
        scratch_shapes=[pltpu.VMEM((CONV_HALO + ts, D), f32), pltpu.VMEM((ts, D), f32),
                        pltpu.VMEM((SUBLANES - 1, CONV_HALO + ts, D), f32)],
        compiler_params=_params(("arbitrary", "arbitrary")),
        name="conv_branch",
    )(x2, mod3, g1, wa, wgt, wg, bg, cw, cb, lng, lnb, wco, bco)


def _qkv_kernel(x_ref, mod_ref, g1_ref, pos_ref, wq_ref, wk_ref, wv_ref, qg_ref, kg_ref,
                inv_ref, m1_ref, m2_ref, bd_ref, q_ref, k_ref, v_ref, *, head_dim):
    hb = _mod_h(x_ref[...], mod_ref, g1_ref, 0, 1).astype(bf16)
    v_ref[...] = jnp.dot(hb, wv_ref[...], preferred_element_type=f32).astype(bf16)

    ang = pos_ref[...] * inv_ref[...]
    cosv = jnp.cos(ang)
    sinv = jnp.sin(ang)
    s_lo = sinv * m1_ref[...]
    s_hi = sinv * m2_ref[...]
    half = head_dim // 8
    bd = bd_ref[...]
    inv_hd = 1.0 / head_dim

    def norm_rope(w_ref, g_ref, o_ref):
        t = jnp.dot(hb, w_ref[...], preferred_element_type=f32)
        for c0 in range(0, t.shape[1], LANES):
            tb = t[:, c0:c0 + LANES]
            ss = jnp.dot((tb * tb).astype(bf16), bd, preferred_element_type=f32)
            tn = tb * lax.rsqrt(ss * inv_hd + EPS) * g_ref[...]
            rot = (tn * cosv + pltpu.roll(tn, LANES - half, 1) * s_lo
                   + pltpu.roll(tn, half, 1) * s_hi)
            o_ref[:, c0:c0 + LANES] = rot.astype(bf16)

    norm_rope(wq_ref, qg_ref, q_ref)
    norm_rope(wk_ref, kg_ref, k_ref)


def _qkv(x2, mod3, g1, pos, wq, wk, wv, qg, kg, inv, m1, m2, bd, B, S, head_dim):
    T, D = x2.shape
    qkw, vw = wq.shape[1], wv.shape[1]
    ts = min(512, S)
    ns = S // ts
    row = lambda b, i: (b * ns + i, 0)
    return pl.pallas_call(
        functools.partial(_qkv_kernel, head_dim=head_dim),
        grid=(B, ns),
        in_specs=[pl.BlockSpec((ts, D), row),
                  pl.BlockSpec((1, 6, D), lambda b, i: (b, 0, 0)),
                  _const_spec((1, D)),
                  pl.BlockSpec((ts, 1), row),
                  _const_spec((D, qkw)), _const_spec((D, qkw)), _const_spec((D, vw)),
                  _const_spec((1, LANES)), _const_spec((1, LANES)), _const_spec((1, LANES)),
                  _const_spec((1, LANES)), _const_spec((1, LANES)), _const_spec((LANES, LANES))],
        out_specs=[pl.BlockSpec((ts, qkw), row), pl.BlockSpec((ts, qkw), row),
                   pl.BlockSpec((ts, vw), row)],
        out_shape=[jax.ShapeDtypeStruct((T, qkw), bf16), jax.ShapeDtypeStruct((T, qkw), bf16),
                   jax.ShapeDtypeStruct((T, vw), bf16)],
        compiler_params=_params(("arbitrary", "arbitrary")),
        name="qkv",
    )(x2, mod3, g1, pos, wq, wk, wv, qg, kg, inv, m1, m2, bd)


def _attn_kernel(q_ref, k_ref, v_ref, lq1_ref, lk1_ref, lq2_ref, lk2_ref, sg_ref, o_ref,
                 m_ref, l_ref, acc_ref, s0_ref, s1_ref, p0_ref, p1_ref, a0_ref, a1_ref,
                 *, lam_init, head_dim, tk):
    tq = q_ref.shape[0]
    nsub = tq // tk
    s_ref, p_ref, a_ref = (s0_ref, s1_ref), (p0_ref, p1_ref), (a0_ref, a1_ref)
    i = pl.program_id(2)

    q = q_ref[...]
    lane = lax.broadcasted_iota(jnp.int32, q.shape, 1)
    zero = jnp.zeros_like(q)
    qs = jnp.concatenate([jnp.where(lane < head_dim, q, zero),
                          jnp.where(lane >= head_dim, q, zero)], axis=0)

    m_ref[...] = jnp.full(m_ref.shape, -jnp.inf, f32)
    l_ref[...] = jnp.zeros(l_ref.shape, f32)
    acc_ref[...] = jnp.zeros(acc_ref.shape, f32)

    def stage_scores(t, slot):
        start = pl.multiple_of(t * tk, tk)
        s_ref[slot][...] = lax.dot_general(qs, k_ref[pl.ds(start, tk), :], (((1,), (1,)), ((), ())),
                                           preferred_element_type=f32)

    def stage_softmax(slot, diag_sub):
        s = s_ref[slot][...]
        if diag_sub is not None:
            r = lax.broadcasted_iota(jnp.int32, (tq, tk), 0)
            c = lax.broadcasted_iota(jnp.int32, (tq, tk), 1) + diag_sub * tk
            keep = c <= r
            keep = jnp.concatenate([keep, keep], axis=0)
            s = jnp.where(keep, s, -jnp.inf)
        mx = s[:, :LANES]
        for c0 in range(LANES, tk, LANES):
            mx = jnp.maximum(mx, s[:, c0:c0 + LANES])
        m_prev = m_ref[...]
        m_new = jnp.maximum(m_prev, jnp.max(mx, axis=-1, keepdims=True))
        alpha = jnp.exp2(m_prev - m_new)
        ps = [jnp.exp2(s[:, c0:c0 + LANES] - m_new) for c0 in range(0, tk, LANES)]
        l_ref[...] = alpha * l_ref[...] + functools.reduce(lambda a, b: a + b, ps)
        p_ref[slot][...] = jnp.concatenate([x.astype(bf16) for x in ps], axis=1)
        a_ref[slot][...] = alpha
        m_ref[...] = m_new

    def stage_pv(t, slot):
        start = pl.multiple_of(t * tk, tk)
        acc_ref[...] = a_ref[slot][...] * acc_ref[...] + jnp.dot(
            p_ref[slot][...], v_ref[pl.ds(start, tk), :], preferred_element_type=f32)

    n_full = i * nsub
    stage_scores(0, 0)

    @pl.when(i > 0)
    def _():
        stage_scores(1, 1)
        stage_softmax(0, None)
        stage_scores(2, 0)
        stage_softmax(1, None)
        stage_pv(0, 0)

        def body(jj, carry):
            tau = 2 * jj
            stage_scores(tau + 1, 1)
            stage_softmax(0, None)
            stage_pv(tau - 1, 1)
            stage_scores(tau + 2, 0)
            stage_softmax(1, None)
            stage_pv(tau, 0)
            return carry
        lax.fori_loop(1, n_full // 2, body, 0)

    for d in range(nsub):
        t = n_full + d
        slot = d % 2
        if d + 1 < nsub:
            stage_scores(t + 1, 1 - slot)
        stage_softmax(slot, d)
        if d == 0:
            @pl.when(i > 0)
            def _():
                stage_pv(t - 1, 1 - slot)
        else:
            stage_pv(t - 1, 1 - slot)
    stage_pv(n_full + nsub - 1, (nsub - 1) % 2)

    lam = (jnp.exp(jnp.sum(lq1_ref[...] * lk1_ref[...], axis=-1, keepdims=True))
           - jnp.exp(jnp.sum(lq2_ref[...] * lk2_ref[...], axis=-1, keepdims=True)) + lam_init)
    o_all = acc_ref[...] / jnp.sum(l_ref[...], axis=-1, keepdims=True)
    o = o_all[:tq] - lam * o_all[tq:]
    o = _rms_rows(o) * sg_ref[...] * (1.0 - lam_init)
    o_ref[...] = o.astype(bf16)


def _attention(q, k, v, lq1, lk1, lq2, lk2, sg, B, S, H, head_dim, lam_init):
    T = q.shape[0]
    tq = min(512, S)
    tk = min(256, tq)
    assert (tq // tk) % 2 == 0
    nq = S // tq
    vd = sg.shape[1]
    return pl.pallas_call(
        functools.partial(_attn_kernel, lam_init=lam_init, head_dim=head_dim, tk=tk),
        grid=(B, H, nq),
        in_specs=[pl.BlockSpec((tq, 2 * head_dim), lambda b, h, i: (b * nq + i, h)),
                  pl.BlockSpec((S, 2 * head_dim), lambda b, h, i: (b, h)),
                  pl.BlockSpec((S, vd), lambda b, h, i: (b, h)),
                  _const_spec((1, head_dim)), _const_spec((1, head_dim)),
                  _const_spec((1, head_dim)), _const_spec((1, head_dim)),
                  _const_spec((1, vd))],
        out_specs=pl.BlockSpec((tq, vd), lambda b, h, i: (b * nq + i, h)),
        out_shape=jax.ShapeDtypeStruct((T, H * vd), bf16),
        scratch_shapes=[pltpu.VMEM((2 * tq, LANES), f32), pltpu.VMEM((2 * tq, LANES), f32),
                        pltpu.VMEM((2 * tq, vd), f32),
                        pltpu.VMEM((2 * tq, tk), f32), pltpu.VMEM((2 * tq, tk), f32),
                        pltpu.VMEM((2 * tq, tk), bf16), pltpu.VMEM((2 * tq, tk), bf16),
                        pltpu.VMEM((2 * tq, LANES), f32), pltpu.VMEM((2 * tq, LANES), f32)],
        compiler_params=_params(("arbitrary", "arbitrary", "arbitrary")),
        name="diff_attn",
    )(q, k, v, lq1, lk1, lq2, lk2, sg)


def _pack_bf16_pair(lo, hi):
    lo_bits = pltpu.bitcast(lo.astype(bf16).astype(f32), jnp.uint32)
    hi_bits = pltpu.bitcast(hi.astype(bf16).astype(f32), jnp.uint32)
    return (lo_bits >> 16) | (hi_bits & jnp.uint32(0xFFFF0000))


def _unpack_bf16_pair(w):
    lo = pltpu.bitcast(w << 16, f32)
    hi = pltpu.bitcast(w & jnp.uint32(0xFFFF0000), f32)
    return lo, hi


def _post_kernel(ao_ref, gy_ref, ga_ref, x_ref, mod_ref, wao_ref, wo_ref, g2_ref, wr_ref, br_ref,
                 x1_ref, h2p_ref, meta_ref, cnt_ref, run_ref):
    tm, d = x_ref.shape

    @pl.when(pl.program_id(0) == 0)
    def _():
        run_ref[...] = jnp.zeros(run_ref.shape, f32)

    ya = jnp.dot(ao_ref[...], wao_ref[...], preferred_element_type=f32)
    mix_in = gy_ref[...].astype(f32) + ga_ref[...].astype(f32) * ya
    mix = jnp.dot(mix_in.astype(bf16), wo_ref[...], preferred_element_type=f32)
    gate1 = mod_ref[0, 2:3, :]
    x1 = x_ref[...] + gate1 * mix
    x1_ref[...] = x1
    h2f = _mod_h(x1, mod_ref, g2_ref, 3, 4)
    h2p_ref[...] = _pack_bf16_pair(h2f[:, :d // 2], h2f[:, d // 2:])

    logits = jnp.dot(h2f.astype(bf16), wr_ref[...], preferred_element_type=f32) + br_ref[...]
    lane = lax.broadcasted_iota(jnp.int32, logits.shape, 1)
    work = logits
    sel, vals = [], []
    for _ in range(TOP_K):
        mx = jnp.max(work, axis=-1, keepdims=True)
        ek = jnp.min(jnp.where(work == mx, lane, LANES), axis=-1, keepdims=True)
        sel.append(ek)
        vals.append(mx)
        work = jnp.where(lane == ek, NEG_BIG, work)
    ex = [jnp.exp(vk - vals[0]) for vk in vals]
    inv_den = 1.0 / functools.reduce(lambda a, b: a + b, ex)

    mask = jnp.zeros(logits.shape, f32)
    for ek in sel:
        mask = jnp.where(lane == ek, 1.0, mask)
    r_i = lax.broadcasted_iota(jnp.int32, (tm, tm), 0)
    c_i = lax.broadcasted_iota(jnp.int32, (tm, tm), 1)
    earlier = jnp.where(c_i < r_i, 1.0, 0.0).astype(bf16)
    rank = jnp.dot(earlier, mask.astype(bf16), preferred_element_type=f32) + run_ref[...]
    run_ref[...] = run_ref[...] + jnp.sum(mask, axis=0, keepdims=True)
    cnt_ref[...] = run_ref[...]

    meta = jnp.zeros(logits.shape, f32)
    for k, (ek, xk) in enumerate(zip(sel, ex)):
        rk = jnp.sum(jnp.where(lane == ek, rank, 0.0), axis=-1, keepdims=True)
        meta = jnp.where(lane == k, ek.astype(f32), meta)
        meta = jnp.where(lane == TOP_K + k, rk, meta)
        meta = jnp.where(lane == 2 * TOP_K + k, xk * inv_den, meta)
    meta_ref[...] = meta


def _post(ao, gy, ga, x2, mod3, wao, wo, g2, wr, br, S):
    T, D = x2.shape
    vw = ao.shape[1]
    tm = min(512, S)
    per_b = S // tm
    row = lambda t: (t, 0)
    return pl.pallas_call(
        _post_kernel,
        grid=(T // tm,),
        in_specs=[pl.BlockSpec((tm, vw), row), pl.BlockSpec((tm, D), row),
                  pl.BlockSpec((tm, D), row), pl.BlockSpec((tm, D), row),
                  pl.BlockSpec((1, 6, D), lambda t: (t // per_b, 0, 0)),
                  _const_spec((vw, D)), _const_spec((D, D)), _const_spec((1, D)),
                  _const_spec((D, LANES)), _const_spec((1, LANES))],
        out_specs=[pl.BlockSpec((tm, D), row), pl.BlockSpec((tm, D // 2), row),
                   pl.BlockSpec((tm, LANES), row), pl.BlockSpec((1, LANES), lambda t: (0, 0))],
        out_shape=[jax.ShapeDtypeStruct((T, D), f32), jax.ShapeDtypeStruct((T, D // 2), jnp.uint32),
                   jax.ShapeDtypeStruct((T, LANES), f32), jax.ShapeDtypeStruct((1, LANES), f32)],
        scratch_shapes=[pltpu.VMEM((1, LANES), f32)],
        compiler_params=_params(("arbitrary",)),
        name="post_router",
    )(ao, gy, ga, x2, mod3, wao, wo, g2, wr, br)


def _split_up_kernel(w_ref, perm_ref, g_ref, l_ref):
    n2 = w_ref.shape[2]
    perm = perm_ref[...]
    for b in range(n2 // (2 * LANES)):
        blk = w_ref[0, :, b * 2 * LANES:(b + 1) * 2 * LANES].astype(bf16)
        r = jnp.dot(blk, perm, preferred_element_type=f32)
        g_ref[0, :, b * LANES:(b + 1) * LANES] = r[:, :LANES].astype(bf16)
        l_ref[0, :, b * LANES:(b + 1) * LANES] = r[:, LANES:].astype(bf16)


def _split_up(w_up):
    E, D, F2 = w_up.shape
    F = F2 // 2
    src = np.arange(2 * LANES)
    dst = np.where(src % 2 == 0, src // 2, LANES + src // 2)
    perm = np.zeros((2 * LANES, 2 * LANES), np.float32)
    perm[src, dst] = 1.0
    ex = lambda e: (e, 0, 0)
    return pl.pallas_call(
        _split_up_kernel,
        grid=(E,),
        in_specs=[pl.BlockSpec((1, D, F2), ex), _const_spec((2 * LANES, 2 * LANES))],
        out_specs=[pl.BlockSpec((1, D, F), ex), pl.BlockSpec((1, D, F), ex)],
        out_shape=[jax.ShapeDtypeStruct((E, D, F), bf16), jax.ShapeDtypeStruct((E, D, F), bf16)],
        compiler_params=_params(("arbitrary",)),
        name="split_up",
    )(w_up, jnp.asarray(perm, bf16))


def _moe_kernel(te_ref, nu_ref, tokc_ref, tokn_ref, dstp_ref, h2p_hbm, wg_ref, wl_ref, bg_ref, bl_ref,
                wd_ref, bd_ref, y_hbm, xbuf, ybuf, gsem, ssem):
    tm = xbuf.shape[1]
    n_tok = h2p_hbm.shape[0]
    i = pl.program_id(0)
    nu = nu_ref[0]
    slot = i % 2
    other = 1 - slot

    def gather_row(tab_ref, s, r):
        code = tab_ref[0, 0, r]
        tok = code & (n_tok - 1) if n_tok & (n_tok - 1) == 0 else lax.rem(code, n_tok)
        pltpu.make_async_copy(h2p_hbm.at[tok], xbuf.at[s, r], gsem.at[s]).start()

    def scatter_row(s, r):
        pltpu.make_async_copy(ybuf.at[s, r], y_hbm.at[dstp_ref[0, 0, r]], ssem.at[s]).start()

    def wait_gather(s):
        pltpu.make_async_copy(h2p_hbm.at[pl.ds(0, tm)], xbuf.at[s], gsem.at[s]).wait()

    def wait_scatter(s):
        pltpu.make_async_copy(ybuf.at[s], y_hbm.at[pl.ds(0, tm)], ssem.at[s]).wait()

    @pl.when(i == 0)
    def _():
        ybuf[...] = jnp.zeros(ybuf.shape, ybuf.dtype)

        def issue(r, carry):
            gather_row(tokc_ref, 0, r)
            return carry
        lax.fori_loop(0, tm, issue, 0, unroll=8)

    @pl.when(i < nu)
    def _():
        wait_gather(slot)

        @pl.when(i >= 1)
        def _():
            wait_scatter(slot)

        for r in range(tm):
            gather_row(tokn_ref, other, r)
            scatter_row(other, r)

        lo, hi = _unpack_bf16_pair(xbuf[slot])
        x = jnp.concatenate([lo, hi], axis=1).astype(bf16)
        glu = jnp.minimum(jnp.dot(x, wg_ref[0], preferred_element_type=f32) + bg_ref[0], SWIGLU_LIMIT)
        lin = jnp.clip(jnp.dot(x, wl_ref[0], preferred_element_type=f32) + bl_ref[0],
                       -SWIGLU_LIMIT, SWIGLU_LIMIT)
        act = glu * jax.nn.sigmoid(SWIGLU_ALPHA * glu) * (lin + 1.0)
        y = jnp.dot(act.astype(bf16), wd_ref[0], preferred_element_type=f32) + bd_ref[0]
        dh = y.shape[1] // 2
        ybuf[slot] = _pack_bf16_pair(y[:, :dh], y[:, dh:])

    @pl.when(i == nu)
    def _():
        def issue(r, carry):
            scatter_row(other, r)
            return carry
        lax.fori_loop(0, tm, issue, 0, unroll=8)
        wait_scatter(other)
        wait_scatter(slot)
        wait_gather(slot)


def _moe_routed(h2p, tile_e, n_used, tab, wg, wl, bg, bl, wd, bdn, n_out_rows, tm):
    T, dh = h2p.shape
    E, D, F = wg.shape
    nt = tab.shape[0] - 1
    ex = lambda i, te, nu: (te[i], 0, 0)
    smem_blk = lambda f: pl.BlockSpec((1, 1, tm), f, memory_space=pltpu.SMEM)
    grid_spec = pltpu.PrefetchScalarGridSpec(
        num_scalar_prefetch=2,
        grid=(nt,),
        in_specs=[smem_blk(lambda i, te, nu: (i + 1, 0, 0)),
                  smem_blk(lambda i, te, nu: (jnp.minimum(i + 2, nt), 0, 0)),
                  smem_blk(lambda i, te, nu: (i, 0, 0)),
                  pl.BlockSpec(memory_space=pl.ANY),
                  pl.BlockSpec((1, D, F), ex), pl.BlockSpec((1, D, F), ex),
                  pl.BlockSpec((1, 1, F), ex), pl.BlockSpec((1, 1, F), ex),
                  pl.BlockSpec((1, F, D), ex), pl.BlockSpec((1, 1, D), ex)],
        out_specs=pl.BlockSpec(memory_space=pl.ANY),
        scratch_shapes=[pltpu.VMEM((2, tm, dh), jnp.uint32), pltpu.VMEM((2, tm, dh), jnp.uint32),
                        pltpu.SemaphoreType.DMA((2,)), pltpu.SemaphoreType.DMA((2,))],
    )
    return pl.pallas_call(
        _moe_kernel,
        grid_spec=grid_spec,
        out_shape=jax.ShapeDtypeStruct((n_out_rows, dh), jnp.uint32),
        compiler_params=_params(("arbitrary",)),
        name="moe_routed",
    )(tile_e, n_used, tab, tab, tab, h2p, wg, wl, bg, bl, wd, bdn)


def _route_tables(meta, counts, E, tm):
    T = meta.shape[0]
    e = meta[:, 0:TOP_K].astype(jnp.int32)
    rk = meta[:, TOP_K:2 * TOP_K].astype(jnp.int32)
    cnt = counts[0, :E].astype(jnp.int32)
    ntile = (cnt + tm - 1) // tm
    tile_end = jnp.cumsum(ntile)
    slot = ((tile_end - ntile) * tm)[e] + rk
    P = T * TOP_K + E * tm
    nt = P // tm
    flat = slot.reshape(-1)
    pair = jnp.arange(T * TOP_K, dtype=jnp.int32)
    tab = (T * TOP_K + jnp.arange(P + tm, dtype=jnp.int32) % tm).at[flat + tm].set(
        (pair % TOP_K) * T + pair // TOP_K, unique_indices=True)
    n_used = tile_end[-1:]
    tiles = jnp.arange(nt, dtype=jnp.int32)
    tile_e = jnp.sum((tiles[:, None] >= tile_end[None, :]).astype(jnp.int32), axis=1)
    last_e = jnp.sum((n_used - 1 >= tile_end).astype(jnp.int32))
    tile_e = jnp.minimum(tile_e, last_e)
    return tile_e, n_used.astype(jnp.int32), tab.reshape(nt + 1, 1, tm)


def _combine_kernel(x1_ref, meta_ref, mod_ref, y0_ref, y1_ref, y2_ref, y3_ref, o_ref):
    dh = y0_ref.shape[1]
    meta = meta_ref[...]
    acc_lo = acc_hi = None
    for k, y_ref in enumerate((y0_ref, y1_ref, y2_ref, y3_ref)):
        w = meta[:, 2 * TOP_K + k:2 * TOP_K + k + 1]
        lo, hi = _unpack_bf16_pair(y_ref[...])
        acc_lo = w * lo if acc_lo is None else acc_lo + w * lo
        acc_hi = w * hi if acc_hi is None else acc_hi + w * hi
    gate2 = mod_ref[0, 5:6, :]
    o_ref[:, :dh] = x1_ref[:, :dh] + gate2[:, :dh] * acc_lo
    o_ref[:, dh:] = x1_ref[:, dh:] + gate2[:, dh:] * acc_hi


def _combine(x1, meta, mod3, y, S):
    T, D = x1.shape
    dh = D // 2
    tm = min(512, S)
    per_b = S // tm
    nb = T // tm
    row = lambda t: (t, 0)
    plane = lambda k: pl.BlockSpec((tm, dh), lambda t: (k * nb + t, 0))
    return pl.pallas_call(
        _combine_kernel,
        grid=(nb,),
        in_specs=[pl.BlockSpec((tm, D), row), pl.BlockSpec((tm, LANES), row),
                  pl.BlockSpec((1, 6, D), lambda t: (t // per_b, 0, 0)),
                  plane(0), plane(1), plane(2), plane(3)],
        out_specs=pl.BlockSpec((tm, D), row),
        out_shape=jax.ShapeDtypeStruct((T, D), f32),
        compiler_params=_params(("arbitrary",)),
        name="combine",
    )(x1, meta, mod3, y, y, y, y)


def _rope_lane_tables(head_dim):
    rot = head_dim // 4
    half = rot // 2
    lane = np.arange(LANES) % head_dim
    inv = np.where(lane < rot, ROPE_THETA ** (-(2.0 * (lane % half)) / rot), 0.0)
    m1 = np.where(lane < half, -1.0, 0.0)
    m2 = np.where((lane >= half) & (lane < rot), 1.0, 0.0)
    blk = np.arange(LANES) // head_dim
    bd = (blk[:, None] == blk[None, :]).astype(np.float32)
    as_row = lambda a: jnp.asarray(a, f32).reshape(1, LANES)
    return as_row(inv), as_row(m1), as_row(m2), jnp.asarray(bd, bf16)


def kernel(x, c, positions, w_ada, b_ada, norm1_g, w_in, conv_w, conv_b, conv_ln_g, conv_ln_b,
           w_conv_out, b_conv_out, q_norm_g, k_norm_g, lambda_q1, lambda_k1, lambda_q2, lambda_k2,
           subln_g, w_attn_out, b_gate, w_o, norm2_g, w_router, b_router, w_up, b_up, w_down, b_down):
    B, S, D = x.shape
    depth = w_ada.shape[0]
    hd = q_norm_g.shape[-1]
    vd = subln_g.shape[-1]
    vw = w_attn_out.shape[1]
    H = vw // vd
    qkw = H * 2 * hd
    E = w_router.shape[-1]
    F = w_down.shape[2]
    dc = conv_w.shape[-1]
    assert 2 * hd == LANES and vd == LANES and dc == D and E <= LANES
    T = B * S

    inv, m1, m2, bd = _rope_lane_tables(hd)
    pos = positions.astype(f32).reshape(T, 1)
    c_in = c
    x2 = x.reshape(T, D)
    for l in range(depth):
        lam_init = 0.8 - 0.6 * math.exp(-0.3 * l)
        mod3 = _ada(c_in, w_ada[l], b_ada[l]).reshape(B, 6, D)
        row = lambda a: a.reshape(1, -1)
        wi = w_in[l].astype(bf16)
        c0 = 2 * dc
        wa, wgt = wi[:, :dc], wi[:, dc:c0]
        wq, wk = wi[:, c0:c0 + qkw], wi[:, c0 + qkw:c0 + 2 * qkw]
        wv = wi[:, c0 + 2 * qkw:c0 + 2 * qkw + vw]
        wg = wi[:, c0 + 2 * qkw + vw:]
        g1 = row(norm1_g[l])

        gy, ga = _conv_branch(x2, mod3, g1, wa, wgt, wg, row(b_gate[l]), conv_w[l], row(conv_b[l]),
                              row(conv_ln_g[l]), row(conv_ln_b[l]), w_conv_out[l].astype(bf16),
                              row(b_conv_out[l]), B, S)
        qg = row(jnp.tile(q_norm_g[l], 2) * (hd ** -0.5 * math.log2(math.e)))
        kg = row(jnp.tile(k_norm_g[l], 2))
        q, k, v = _qkv(x2, mod3, g1, pos, wq, wk, wv, qg, kg, inv, m1, m2, bd, B, S, hd)
        ao = _attention(q, k, v, row(lambda_q1[l]), row(lambda_k1[l]), row(lambda_q2[l]),
                        row(lambda_k2[l]), row(subln_g[l]), B, S, H, hd, lam_init)

        wr = jnp.zeros((D, LANES), f32).at[:, :E].set(w_router[l]).astype(bf16)
        br = jnp.full((1, LANES), NEG_BIG, f32).at[0, :E].set(b_router[l])
        x1, h2p, meta, counts = _post(ao, gy, ga, x2, mod3, w_attn_out[l].astype(bf16),
                                      w_o[l].astype(bf16), row(norm2_g[l]), wr, br, S)

        tm_e = min(512, T)
        tile_e, n_used, tab = _route_tables(meta, counts, E, tm_e)
        wu_g, wu_l = _split_up(w_up[l])
        y = _moe_routed(h2p, tile_e, n_used, tab, wu_g, wu_l,
                        b_up[l][:, None, 0::2], b_up[l][:, None, 1::2],
                        w_down[l].astype(bf16), b_down[l][:, None, :], T * TOP_K + tm_e, tm_e)
        x2 = _combine(x1, meta, mod3, y, S)
    return x2.reshape(B, S, D)
```
